```python
import jax
import jax.numpy as jnp
from jax import lax
import numpy as np

D_MODEL = 1024
BATCH = 16
SEQ = 2048
DEPTH = 2

N_MIXERS = 2
N_FNET = (DEPTH + 1) // 2
N_RWKV = DEPTH // 2
FNET_GROUPS = 4
FNET_GROUP_DIM = D_MODEL // FNET_GROUPS
HEAD_DIM = 64
N_HEADS = D_MODEL // HEAD_DIM
N_DIR = 2
R_DECAY = max(32, int(round(1.8 * D_MODEL ** 0.5 / 32)) * 32)
R_AAA = max(32, int(round(1.8 * D_MODEL ** 0.5 / 32)) * 32)
R_GATE = max(32, int(round(0.6 * D_MODEL ** 0.8 / 32)) * 32)
D_FF = 256 * ((8 * D_MODEL // 3 + 255) // 256)
CONV_WIDTH = 3
RMS_EPS = 1e-6
GN_EPS = 64e-5
L2_EPS = 1e-12

kernel_name = 'hybrid_fnet_rwkv7_encoder'


def rms_norm(x, g):
    xf = x.astype(jnp.float32)
    y = xf * lax.rsqrt(jnp.mean(xf * xf, axis=-1, keepdims=True) + RMS_EPS)
    return (y * g.astype(jnp.float32)).astype(x.dtype)


def shift_prev(t):
    return jnp.pad(t, ((0, 0), (1, 0), (0, 0)))[:, :-1]


def shift_next(t):
    return jnp.pad(t, ((0, 0), (0, 1), (0, 0)))[:, 1:]


def fourier_mixer(h, w_o, b_o):
    b, s, d = h.shape
    hg = h.astype(jnp.float32).reshape(b, s, FNET_GROUPS, FNET_GROUP_DIM)
    f = jnp.fft.fftn(hg, axes=(1, 3), norm='ortho').real
    return f.reshape(b, s, d).astype(h.dtype) @ w_o + b_o


def wkv7_scan(r, w, k, v, a, bb):
    def step(state, inp):
        r_t, w_t, k_t, v_t, a_t, b_t = inp
        sa = jnp.einsum('zbhij,zbhj->zbhi', state, a_t)
        state = (state * w_t[..., None, :]
                 + sa[..., :, None] * b_t[..., None, :]
                 + v_t[..., :, None] * k_t[..., None, :])
        y_t = jnp.einsum('zbhij,zbhj->zbhi', state, r_t)
        return state, y_t
    _, dirs, bsz, nh, n = r.shape
    state0 = jnp.zeros((dirs, bsz, nh, n, n), jnp.float32)
    _, y = lax.scan(step, state0, (r, w, k, v, a, bb))
    return y


def rwkv7_mixer(h, mu, w_rkv, w0, w1, w2, a0, a1, a2, g1, g2, k_k, k_a, r_k, ln_w, ln_b, w_o):
    bsz, s, d = h.shape
    d_prev = shift_prev(h) - h
    d_next = shift_next(h) - h

    def mix(i):
        return h + d_prev * mu[0, i] + d_next * mu[1, i]

    f32 = lambda t: t.astype(jnp.float32)
    heads = lambda t: t.reshape(t.shape[:-1] + (N_HEADS, HEAD_DIM))
    r = f32(mix(0) @ w_rkv[0])
    k = f32(mix(1) @ w_rkv[1])
    v = f32(mix(2) @ w_rkv[2])
    xw, xa, xg = mix(3), mix(4), mix(5)
    lw = jnp.einsum('zbsr,zrd->zbsd', jnp.tanh(jnp.einsum('bsd,zdr->zbsr', xw, w1)), w2)
    w_log = -jax.nn.softplus(-(f32(w0)[:, None, None, :] + f32(lw))) - 0.5
    decay = jnp.exp(-jnp.exp(w_log))
    la = jnp.einsum('zbsr,zrd->zbsd', jnp.einsum('bsd,zdr->zbsr', xa, a1), a2)
    a_gate = jax.nn.sigmoid(f32(a0)[:, None, None, :] + f32(la))
    g = jax.nn.sigmoid(xg @ g1) @ g2
    kk = heads(k * f32(k_k))
    kk = kk / jnp.maximum(jnp.linalg.norm(kk, axis=-1, keepdims=True), L2_EPS)
    k_dir = heads(k[None] * (1.0 + (a_gate - 1.0) * f32(k_a)))
    b_dir = kk[None] * heads(a_gate)
    both = lambda t: jnp.stack([t, jnp.flip(t, axis=1)])
    orient = lambda t: jnp.stack([t[0], jnp.flip(t[1], axis=1)])
    tm = lambda t: jnp.moveaxis(t, 2, 0)
    y = wkv7_scan(tm(both(heads(r))), tm(orient(heads(decay))), tm(orient(k_dir)),
                  tm(both(heads(v))), tm(both(-kk)), tm(orient(b_dir)))
    y = jnp.moveaxis(y, 0, 2)
    y = y[0] + jnp.flip(y[1], axis=1)
    mean = jnp.mean(y, axis=-1, keepdims=True)
    var = jnp.mean(jnp.square(y - mean), axis=-1, keepdims=True)
    y = (y - mean) * lax.rsqrt(var + GN_EPS) * heads(f32(ln_w)) + heads(f32(ln_b))
    bonus = jnp.sum(heads(r) * (k_dir[0] + k_dir[1]) * f32(r_k), axis=-1, keepdims=True) * heads(v)
    out = (y + bonus).reshape(bsz, s, d).astype(h.dtype) * g
    return out @ w_o


def conv_ffn(h, w_in, conv_w, conv_b, w_out):
    u = h @ w_in
    u = shift_prev(u) * conv_w[0] + u * conv_w[1] + shift_next(u) * conv_w[2] + conv_b
    gate, val = jnp.split(u, 2, axis=-1)
    return (jax.nn.silu(gate) * val) @ w_out


def setup_inputs(seed: int = 0) -> dict:
    key = jax.random.key(seed)
    ks = iter(jax.random.split(key, 32))
    nrm = lambda shape, scale: jax.random.normal(next(ks), shape, jnp.float32) * scale
    uni = lambda shape, lo, hi: jax.random.uniform(next(ks), shape, jnp.float32, lo, hi)
    d = D_MODEL
    return {
        'x': nrm((BATCH, SEQ, d), 1.0),
        'norm_mix_g': 1.0 + nrm((DEPTH, d), 0.05),
        'norm_ffn_g': 1.0 + nrm((DEPTH, d), 0.05),
        'norm_final_g': 1.0 + nrm((d,), 0.05),
        'fnet_w_o': nrm((N_FNET, d, d), d ** -0.5),
        'fnet_b_o': nrm((N_FNET, d), 0.01),
        'rwkv_mu': uni((N_RWKV, 2, 6, d), 0.0, 0.5),
        'rwkv_w_rkv': nrm((N_RWKV, 3, d, d), d ** -0.5),
        'rwkv_w0': uni((N_RWKV, N_DIR, d), -6.0, -1.0),
        'rwkv_w1': nrm((N_RWKV, N_DIR, d, R_DECAY), d ** -0.5),
        'rwkv_w2': nrm((N_RWKV, N_DIR, R_DECAY, d), 0.1 * R_DECAY ** -0.5),
        'rwkv_a0': nrm((N_RWKV, N_DIR, d), 0.1),
        'rwkv_a1': nrm((N_RWKV, N_DIR, d, R_AAA), d ** -0.5),
        'rwkv_a2': nrm((N_RWKV, N_DIR, R_AAA, d), R_AAA ** -0.5),
        'rwkv_g1': nrm((N_RWKV, d, R_GATE), d ** -0.5),
        'rwkv_g2': nrm((N_RWKV, R_GATE, d), R_GATE ** -0.5),
        'rwkv_k_k': 0.85 + nrm((N_RWKV, d), 0.05),
        'rwkv_k_a': 1.0 + nrm((N_RWKV, d), 0.05),
        'rwkv_r_k': nrm((N_RWKV, N_HEADS, HEAD_DIM), 0.1),
        'rwkv_ln_w': 1.0 + nrm((N_RWKV, d), 0.05),
        'rwkv_ln_b': nrm((N_RWKV, d), 0.01),
        'rwkv_w_o': nrm((N_RWKV, d, d), d ** -0.5),
        'ffn_w_in': nrm((DEPTH, d, 2 * D_FF), d ** -0.5),
        'ffn_conv_w': nrm((DEPTH, CONV_WIDTH, 2 * D_FF), CONV_WIDTH ** -0.5),
        'ffn_conv_b': nrm((DEPTH, 2 * D_FF), 0.01),
        'ffn_w_out': nrm((DEPTH, D_FF, d), D_FF ** -0.5),
    }


def reference(x, norm_mix_g, norm_ffn_g, norm_final_g, fnet_w_o, fnet_b_o, rwkv_mu, rwkv_w_rkv,
              rwkv_w0, rwkv_w1, rwkv_w2, rwkv_a0, rwkv_a1, rwkv_a2, rwkv_g1, rwkv_g2, rwkv_k_k,
              rwkv_k_a, rwkv_r_k, rwkv_ln_w, rwkv_ln_b, rwkv_w_o, ffn_w_in, ffn_conv_w, ffn_conv_b,
              ffn_w_out):
    h = x
    for i in range(DEPTH):
        hn = rms_norm(h, norm_mix_g[i])
        j = i // N_MIXERS
        if i % N_MIXERS == 0:
            h = h + fourier_mixer(hn, fnet_w_o[j], fnet_b_o[j])
        else:
            h = h + rwkv7_mixer(hn, rwkv_mu[j], rwkv_w_rkv[j], rwkv_w0[j], rwkv_w1[j], rwkv_w2[j],
                                rwkv_a0[j], rwkv_a1[j], rwkv_a2[j], rwkv_g1[j], rwkv_g2[j],
                                rwkv_k_k[j], rwkv_k_a[j], rwkv_r_k[j], rwkv_ln_w[j], rwkv_ln_b[j],
                                rwkv_w_o[j])
        h = h + conv_ffn(rms_norm(h, norm_ffn_g[i]), ffn_w_in[i], ffn_conv_w[i], ffn_conv_b[i],
                         ffn_w_out[i])
    return rms_norm(h, norm_final_g)
```

```python
import functools
import math

import jax
import jax.numpy as jnp
from jax import lax
from jax.experimental import pallas as pl
from jax.experimental.pallas import tpu as pltpu

HEAD_DIM = 64
FNET_GROUPS = 4
RMS_EPS = 1e-6
GN_EPS = 64e-5
L2_EPS = 1e-12

LANES = 128
SUBLANES = 8
WKV_CHUNK = 64
VMEM_LIMIT_BYTES = 56 * 1024 * 1024

F32 = jnp.float32
BF16 = jnp.bfloat16


def _params(semantics):
    return pltpu.CompilerParams(dimension_semantics=semantics, vmem_limit_bytes=VMEM_LIMIT_BYTES)


def _rms(x, g):
    ms = jnp.mean(x * x, axis=-1, keepdims=True)
    return x * lax.rsqrt(ms + RMS_EPS) * g


def _dot(a, b):
    return jnp.dot(a, b, preferred_element_type=F32)


def _dot_nt(a, b):
    return lax.dot_general(a, b, (((1,), (1,)), ((), ())), preferred_element_type=F32)


def _dot_tn(a, b):
    return lax.dot_general(a, b, (((0,), (0,)), ((), ())), preferred_element_type=F32)


def _fnet_a_kernel(x_ref, g_ref, cs_ref, zc_ref, zs_ref):
    hn = _rms(x_ref[...], g_ref[...]).astype(BF16)
    gd = cs_ref.shape[0]
    for grp in range(hn.shape[1] // gd):
        sl = slice(grp * gd, (grp + 1) * gd)
        z = _dot(hn[:, sl], cs_ref[...])
        zc_ref[:, sl] = z[:, :gd].astype(BF16)
        zs_ref[:, sl] = z[:, gd:].astype(BF16)


def _fnet_b_kernel(c_ref, s_ref, zc_ref, zs_ref, wo_ref, bo_ref, x_ref, o_ref, *, scale):
    f = _dot(c_ref[...], zc_ref[...]) - _dot(s_ref[...], zs_ref[...])
    f = (f * scale).astype(BF16)
    o_ref[...] = x_ref[...] + _dot(f, wo_ref[...]) + bo_ref[...]


def _dft_mats(n):
    idx = lax.iota(jnp.int32, n)
    m = (idx[:, None] * idx[None, :]) % n
    ang = m.astype(F32) * F32(2.0 * math.pi / n)
    return jnp.cos(ang), jnp.sin(ang)


def _fnet_layer(x, g, w_o, b_o):
    b, s, d = x.shape
    gd = d // FNET_GROUPS
    t = b * s
    cc, sc = _dft_mats(gd)
    cs = jnp.concatenate([cc, sc], axis=1).astype(BF16)
    cs_seq, ss_seq = _dft_mats(s)
    cs_seq = cs_seq.astype(BF16)
    ss_seq = ss_seq.astype(BF16)
    tm = min(1024, t)
    zc, zs = pl.pallas_call(
        _fnet_a_kernel,
        grid=(t // tm,),
        in_specs=[pl.BlockSpec((tm, d), lambda i: (i, 0)),
                  pl.BlockSpec((1, d), lambda i: (0, 0)),
                  pl.BlockSpec((gd, 2 * gd), lambda i: (0, 0))],
        out_specs=[pl.BlockSpec((tm, d), lambda i: (i, 0)),
                   pl.BlockSpec((tm, d), lambda i: (i, 0))],
        out_shape=[jax.ShapeDtypeStruct((t, d), BF16), jax.ShapeDtypeStruct((t, d), BF16)],
        compiler_params=_params(("parallel",)),
        name="fnet_channel_dft",
    )(x.reshape(t, d), g.reshape(1, d), cs)
    zc = zc.reshape(b, s, d)
    zs = zs.reshape(b, s, d)
    ts = min(512, s)
    scale = 1.0 / math.sqrt(s * gd)
    return pl.pallas_call(
        functools.partial(_fnet_b_kernel, scale=scale),
        grid=(b, s // ts),
        in_specs=[pl.BlockSpec((ts, s), lambda bi, i: (i, 0)),
                  pl.BlockSpec((ts, s), lambda bi, i: (i, 0)),
                  pl.BlockSpec((None, s, d), lambda bi, i: (bi, 0, 0)),
                  pl.BlockSpec((None, s, d), lambda bi, i: (bi, 0, 0)),
                  pl.BlockSpec((d, d), lambda bi, i: (0, 0)),
                  pl.BlockSpec((1, d), lambda bi, i: (0, 0)),
                  pl.BlockSpec((None, ts, d), lambda bi, i: (bi, i, 0))],
        out_specs=pl.BlockSpec((None, ts, d), lambda bi, i: (bi, i, 0)),
        out_shape=jax.ShapeDtypeStruct((b, s, d), F32),
        compiler_params=_params(("parallel", "arbitrary")),
        name="fnet_seq_dft_proj",
    )(cs_seq, ss_seq, zc, zs, w_o.astype(BF16), b_o.reshape(1, d), x)


def _convffn_kernel(h_ref, g_ref, wg_ref, wv_ref, cwg_ref, cwv_ref, cbg_ref, cbv_ref, wo_ref,
                    gf_ref, o_ref, hn_ref, *, final_norm):
    j = pl.program_id(1)

    @pl.when(j == 0)
    def _():
        h = h_ref[...]
        hn_ref[...] = _rms(h, g_ref[...]).astype(BF16)
        o_ref[...] = h

    hn = hn_ref[...]
    s = hn.shape[0]
    row = lax.broadcasted_iota(jnp.int32, (s, 1), 0)
    first = row == 0
    last = row == s - 1

    def conv(u, cw_ref, cb_ref):
        up = jnp.where(first, 0.0, pltpu.roll(u, 1, 0))
        un = jnp.where(last, 0.0, pltpu.roll(u, s - 1, 0))
        cw = cw_ref[...]
        return up * cw[0:1] + u * cw[1:2] + un * cw[2:3] + cb_ref[...]

    ug = conv(_dot(hn, wg_ref[...]), cwg_ref, cbg_ref)
    uv = conv(_dot(hn, wv_ref[...]), cwv_ref, cbv_ref)
    act = (ug * jax.nn.sigmoid(ug) * uv).astype(BF16)
    o_ref[...] += _dot(act, wo_ref[...])

    if final_norm:
        @pl.when(j == pl.num_programs(1) - 1)
        def _():
            o_ref[...] = _rms(o_ref[...], gf_ref[...])


def _convffn_layer(h, g, w_in, conv_w, conv_b, w_out, g_final=None):
    b, s, d = h.shape
    dff = w_out.shape[0]
    ck = 256 if dff % 256 == 0 else dff
    nj = dff // ck
    w_in = w_in.astype(BF16)
    conv_b = conv_b.reshape(1, 2 * dff)
    final_norm = g_final is not None
    gf = (g_final if final_norm else g).reshape(1, d)
    return pl.pallas_call(
        functools.partial(_convffn_kernel, final_norm=final_norm),
        grid=(b, nj),
        in_specs=[pl.BlockSpec((None, s, d), lambda bi, j: (bi, 0, 0)),
                  pl.BlockSpec((1, d), lambda bi, j: (0, 0)),
                  pl.BlockSpec((d, ck), lambda bi, j: (0, j)),
                  pl.BlockSpec((d, ck), lambda bi, j: (0, nj + j)),
                  pl.BlockSpec((3, ck), lambda bi, j: (0, j)),
                  pl.BlockSpec((3, ck), lambda bi, j: (0, nj + j)),
                  pl.BlockSpec((1, ck), lambda bi, j: (0, j)),
                  pl.BlockSpec((1, ck), lambda bi, j: (0, nj + j)),
                  pl.BlockSpec((ck, d), lambda bi, j: (j, 0)),
                  pl.BlockSpec((1, d), lambda bi, j: (0, 0))],
        out_specs=pl.BlockSpec((None, s, d), lambda bi, j: (bi, 0, 0)),
        out_shape=jax.ShapeDtypeStruct((b, s, d), F32),
        scratch_shapes=[pltpu.VMEM((s, d), BF16)],
        compiler_params=_params(("parallel", "arbitrary")),
        name="convffn_final" if final_norm else "convffn",
    )(h, g.reshape(1, d), w_in, w_in, conv_w, conv_w, conv_b, conv_b, w_out.astype(BF16), gf)


def _rwkv_pre_kernel(h_ref, hp_ref, hx_ref, g_ref, mu_ref, wrkv_ref, w1_ref, w2_ref, w0_ref,
                     a1_ref, a2_ref, a0_ref, g1_ref, g2_ref, kk_ref, ka_ref, rk_ref, hsum_ref,
                     r_out, v_out, kkn_out, lw_out, kd_out, bd_out, bonus_out, gate_out):
    i = pl.program_id(1)
    g = g_ref[...]
    hn = _rms(h_ref[...], g)
    ts = hn.shape[0]
    prev_edge = _rms(hp_ref[...], g)[SUBLANES - 1:SUBLANES]
    next_edge = _rms(hx_ref[...], g)[0:1]
    prev_edge = jnp.where(i == 0, 0.0, prev_edge)
    next_edge = jnp.where(i == pl.num_programs(1) - 1, 0.0, next_edge)
    row = lax.broadcasted_iota(jnp.int32, (ts, 1), 0)
    prev = jnp.where(row == 0, prev_edge, pltpu.roll(hn, 1, 0))
    nxt = jnp.where(row == ts - 1, next_edge, pltpu.roll(hn, ts - 1, 0))
    dp = prev - hn
    dn = nxt - hn
    mu = mu_ref[...]
    n_mix = mu.shape[0] // 2

    def mix(m):
        return (hn + dp * mu[m:m + 1] + dn * mu[n_mix + m:n_mix + m + 1]).astype(BF16)

    r = _dot(mix(0), wrkv_ref[0])
    k = _dot(mix(1), wrkv_ref[1])
    v = _dot(mix(2), wrkv_ref[2])
    tw = jnp.tanh(_dot(mix(3), w1_ref[...])).astype(BF16)
    ta = _dot(mix(4), a1_ref[...]).astype(BF16)
    gate = _dot(jax.nn.sigmoid(_dot(mix(5), g1_ref[...])).astype(BF16), g2_ref[...])

    kraw = k * kk_ref[...]
    ssq = _dot((kraw * kraw).astype(BF16), hsum_ref[...])
    kkn = kraw * lax.rsqrt(jnp.maximum(ssq, L2_EPS * L2_EPS))
    ka = ka_ref[...]
    kd_sum = jnp.zeros_like(k)
    for zdir in range(2):
        xw = w0_ref[zdir:zdir + 1] + _dot(tw, w2_ref[zdir])
        w_log = -(jnp.maximum(-xw, 0.0) + jnp.log(1.0 + jnp.exp(-jnp.abs(xw)))) - 0.5
        lw_out[zdir] = -jnp.exp(w_log)
        ag = jax.nn.sigmoid(a0_ref[zdir:zdir + 1] + _dot(ta, a2_ref[zdir]))
        kd = k * (1.0 + (ag - 1.0) * ka)
        kd_sum = kd_sum + kd
        kd_out[zdir] = kd.astype(BF16)
        bd_out[zdir] = (kkn * ag).astype(BF16)
    bonus = _dot((r * kd_sum * rk_ref[...]).astype(BF16), hsum_ref[...]) * v
    r_out[...] = r.astype(BF16)
    v_out[...] = v.astype(BF16)
    kkn_out[...] = kkn.astype(BF16)
    bonus_out[...] = bonus
    gate_out[...] = gate


def _rwkv_pre(h, g, mu, w_rkv, w0, w1, w2, a0, a1, a2, g1, g2, k_k, k_a, r_k, hsum):
    b, s, d = h.shape
    ts = min(256, s)
    nt = s // ts
    nh8 = s // SUBLANES
    rdec = w1.shape[-1]
    raaa = a1.shape[-1]
    w1c = jnp.concatenate([w1[0], w1[1]], axis=1).astype(BF16)
    a1c = jnp.concatenate([a1[0], a1[1]], axis=1).astype(BF16)
    zw = jnp.zeros_like(w2[0])
    za = jnp.zeros_like(a2[0])
    w2p = jnp.stack([jnp.concatenate([w2[0], zw], 0), jnp.concatenate([zw, w2[1]], 0)]).astype(BF16)
    a2p = jnp.stack([jnp.concatenate([a2[0], za], 0), jnp.concatenate([za, a2[1]], 0)]).astype(BF16)
    full2 = lambda shape: pl.BlockSpec(shape, lambda bi, i: (0,) * len(shape))
    tile = pl.BlockSpec((None, ts, d), lambda bi, i: (bi, i, 0))
    tile2 = pl.BlockSpec((2, None, ts, d), lambda bi, i: (0, bi, i, 0))
    tpb = ts // SUBLANES
    in_specs = [
        tile,
        pl.BlockSpec((None, SUBLANES, d), lambda bi, i: (bi, jnp.maximum(i * tpb - 1, 0), 0)),
        pl.BlockSpec((None, SUBLANES, d), lambda bi, i: (bi, jnp.minimum((i + 1) * tpb, nh8 - 1), 0)),
        full2((1, d)), full2((2 * mu.shape[1], d)), full2((3, d, d)),
        full2((d, 2 * rdec)), full2((2, 2 * rdec, d)), full2((2, d)),
        full2((d, 2 * raaa)), full2((2, 2 * raaa, d)), full2((2, d)),
        full2(g1.shape), full2(g2.shape), full2((1, d)), full2((1, d)), full2((1, d)), full2((d, d)),
    ]
    out_specs = [tile, tile, tile, tile2, tile2, tile2, tile, tile]
    bsd = (b, s, d)
    out_shape = [jax.ShapeDtypeStruct(bsd, BF16), jax.ShapeDtypeStruct(bsd, BF16),
                 jax.ShapeDtypeStruct(bsd, BF16), jax.ShapeDtypeStruct((2,) + bsd, F32),
                 jax.ShapeDtypeStruct((2,) + bsd, BF16), jax.ShapeDtypeStruct((2,) + bsd, BF16),
                 jax.ShapeDtypeStruct(bsd, F32), jax.ShapeDtypeStruct(bsd, F32)]
    return pl.pallas_call(
        _rwkv_pre_kernel,
        grid=(b, nt),
        in_specs=in_specs,
        out_specs=out_specs,
        out_shape=out_shape,
        compiler_params=_params(("parallel", "arbitrary")),
        name="rwkv_projections",
    )(h, h, h, g.reshape(1, d), mu.reshape(2 * mu.shape[1], d), w_rkv.astype(BF16),
      w1c, w2p, w0, a1c, a2p, a0, g1.astype(BF16), g2.astype(BF16),
      k_k.reshape(1, d), k_a.reshape(1, d), r_k.reshape(1, d), hsum)


def _wkv_kernel(r_ref, v_ref, kk_ref, lw_ref, kd_ref, bd_ref, y_ref, st_ref):
    zdir = pl.program_id(1)
    c = pl.program_id(2)
    ch = r_ref.shape[0]
    pr = 2 * ch
    n_pairs = r_ref.shape[1] // LANES
    fwd = zdir == 0

    @pl.when(c == 0)
    def _():
        st_ref[...] = jnp.zeros_like(st_ref)

    ri = lax.broadcasted_iota(jnp.int32, (pr, pr), 0)
    ci = lax.broadcasted_iota(jnp.int32, (pr, pr), 1)
    head_r = jnp.where(ri >= ch, 1, 0)
    head_c = jnp.where(ci >= ch, 1, 0)
    same_head = head_r == head_c
    tt = ri - ch * head_r
    jj = ci - ch * head_c
    sign = jnp.where(fwd, 1, -1)
    order = (tt - jj) * sign
    strict = same_head & (order > 0)
    incl = same_head & (order >= 0)
    eye = (ri == ci).astype(F32)
    r8 = lax.broadcasted_iota(jnp.int32, (ch, ch), 0)
    c8 = lax.broadcasted_iota(jnp.int32, (ch, ch), 1)
    tri = jnp.where((r8 - c8) * sign >= 0, 1.0, 0.0).astype(BF16)

    lw = lw_ref[...]
    lw_hi = lw.astype(BF16)
    lw_lo = (lw - lw_hi.astype(F32)).astype(BF16)
    cum = _dot(tri, lw_hi) + _dot(tri, lw_lo)
    tot = jnp.where(fwd, cum[ch - 1:ch], cum[0:1])
    e_in = jnp.exp(cum)
    e_ex = jnp.exp(cum - lw)
    e_neg = jnp.exp(-cum)
    g_tot = jnp.exp(tot)
    r_t = r_ref[...].astype(F32) * e_in
    a_t = -(kk_ref[...].astype(F32) * e_ex)
    k_t = kd_ref[...].astype(F32) * e_neg
    b_t = bd_ref[...].astype(F32) * e_neg
    k_h = k_t * g_tot
    b_h = b_t * g_tot
    v_all = v_ref[...]

    head0 = lax.broadcasted_iota(jnp.int32, (1, LANES), 1) < HEAD_DIM

    def blockdiag(x):
        return jnp.concatenate([jnp.where(head0, x, 0.0), jnp.where(head0, 0.0, x)], axis=0).astype(BF16)

    n_levels = int(math.log2(ch))
    for p in range(n_pairs):
        sl = slice(p * LANES, (p + 1) * LANES)
        lr = jnp.concatenate([blockdiag(a_t[:, sl]), blockdiag(r_t[:, sl])], axis=0)
        bk = jnp.concatenate([blockdiag(b_t[:, sl]), blockdiag(k_t[:, sl])], axis=0)
        bkh = jnp.concatenate([blockdiag(b_h[:, sl]), blockdiag(k_h[:, sl])], axis=0)
        vb = blockdiag(v_all[:, sl].astype(F32))
        sc = _dot_nt(lr, bk)
        lab = jnp.where(strict, sc[:pr, :pr], 0.0)
        lak = jnp.where(strict, sc[:pr, pr:], 0.0)
        mrb = jnp.where(incl, sc[pr:, :pr], 0.0)
        mrk = jnp.where(incl, sc[pr:, pr:], 0.0)
        t_inv = eye + lab
        lab_b = lab.astype(BF16)
        pk = _dot(lab_b, lab_b)
        for lvl in range(1, n_levels):
            pb = pk.astype(BF16)
            if lvl < n_levels - 1:
                tp = _dot(jnp.concatenate([t_inv.astype(BF16), pb], axis=0), pb)
                t_inv = t_inv + tp[:pr]
                pk = tp[pr:]
            else:
                t_inv = t_inv + _dot(t_inv.astype(BF16), pb)
        st = st_ref[p]
        lrs = _dot_nt(lr, st.astype(BF16))
        w2 = lrs[:pr] + _dot(lak.astype(BF16), vb)
        u = _dot(t_inv.astype(BF16), w2.astype(BF16))
        uv = jnp.concatenate([u.astype(BF16), vb], axis=0)
        y = lrs[pr:] + _dot(jnp.concatenate([mrb.astype(BF16), mrk.astype(BF16)], axis=1), uv)
        y_ref[:, sl] = y[:ch] + y[ch:]
        st_ref[p] = st * g_tot[:, sl] + _dot_tn(uv, bkh)


def _wkv(r, v, kkn, lw, kd, bd):
    b, s, d = r.shape
    ch = min(WKV_CHUNK, s)
    nc = s // ch
    n_pairs = d // LANES

    def cidx(zdir, c):
        return jnp.where(zdir == 0, c, nc - 1 - c)

    shared = pl.BlockSpec((None, ch, d), lambda bi, zdir, c: (bi, cidx(zdir, c), 0))
    perdir = pl.BlockSpec((None, None, ch, d), lambda bi, zdir, c: (zdir, bi, cidx(zdir, c), 0))
    return pl.pallas_call(
        _wkv_kernel,
        grid=(b, 2, nc),
        in_specs=[shared, shared, shared, perdir, perdir, perdir],
        out_specs=perdir,
        out_shape=jax.ShapeDtypeStruct((2, b, s, d), F32),
        scratch_shapes=[pltpu.VMEM((n_pairs, LANES, LANES), F32)],
        compiler_params=_params(("parallel", "parallel", "arbitrary")),
        name="wkv7_chunked_scan",
    )(r, v, kkn, lw, kd, bd)


def _rwkv_post_kernel(y_ref, bonus_ref, gate_ref, h_ref, lnw_ref, lnb_ref, hmean_ref, wo_ref, o_ref):
    y = y_ref[0] + y_ref[1]
    mean = _dot(y.astype(BF16), hmean_ref[...])
    dlt = y - mean
    var = _dot((dlt * dlt).astype(BF16), hmean_ref[...])
    yn = dlt * lax.rsqrt(var + GN_EPS) * lnw_ref[...] + lnb_ref[...]
    out = ((yn + bonus_ref[...]) * gate_ref[...]).astype(BF16)
    o_ref[...] = h_ref[...] + _dot(out, wo_ref[...])


def _rwkv_post(y, bonus, gate, h, ln_w, ln_b, hmean, w_o):
    b, s, d = h.shape
    ts = min(512, s)
    tile = pl.BlockSpec((None, ts, d), lambda bi, i: (bi, i, 0))
    vec = pl.BlockSpec((1, d), lambda bi, i: (0, 0))
    mat = pl.BlockSpec((d, d), lambda bi, i: (0, 0))
    return pl.pallas_call(
        _rwkv_post_kernel,
        grid=(b, s // ts),
        in_specs=[pl.BlockSpec((2, None, ts, d), lambda bi, i: (0, bi, i, 0)),
                  tile, tile, tile, vec, vec, mat, mat],
        out_specs=tile,
        out_shape=jax.ShapeDtypeStruct((b, s, d), F32),
        compiler_params=_params(("parallel", "arbitrary")),
        name="rwkv_groupnorm_proj",
    )(y, bonus, gate, h, ln_w.reshape(1, d), ln_b.reshape(1, d), hmean, w_o.astype(BF16))


def _rwkv_layer(h, g, mu, w_rkv, w0, w1, w2, a0, a1, a2, g1, g2, k_k, k_a, r_k, ln_w, ln_b, w_o):
    d = h.shape[-1]
    head_of = lax.iota(jnp.int32, d) // HEAD_DIM
    hsum = (head_of[:, None] == head_of[None, :]).astype(BF16)
    hmean = (hsum.astype(F32) / HEAD_DIM).astype(BF16)
    r, v, kkn, lw, kd, bd, bonus, gate = _rwkv_pre(
        h, g, mu, w_rkv, w0, w1, w2, a0, a1, a2, g1, g2, k_k, k_a, r_k, hsum)
    y = _wkv(r, v, kkn, lw, kd, bd)
    return _rwkv_post(y, bonus, gate, h, ln_w, ln_b, hmean, w_o)


def kernel(x, norm_mix_g, norm_ffn_g, norm_final_g, fnet_w_o, fnet_b_o, rwkv_mu, rwkv_w_rkv, rwkv_w0, rwkv_w1, rwkv_w2, rwkv_a0, rwkv_a1, rwkv_a2, rwkv_g1, rwkv_g2, rwkv_k_k, rwkv_k_a, rwkv_r_k, rwkv_ln_w, rwkv_ln_b, rwkv_w_o, ffn_w_in, ffn_conv_w, ffn_conv_b, ffn_w_out):
    depth = norm_mix_g.shape[0]
    h = x
    for i in range(depth):
        j = i // 2
        if i % 2 == 0:
            h = _fnet_layer(h, norm_mix_g[i], fnet_w_o[j], fnet_b_o[j])
        else:
            h = _rwkv_layer(h, norm_mix_g[i], rwkv_mu[j], rwkv_w_rkv[j], rwkv_w0[j], rwkv_w1[j],
                            rwkv_w2[j], rwkv_a0[j], rwkv_a1[j], rwkv_a2[j], rwkv_g1[j], rwkv_g2[j],
                            rwkv_k_k[j], rwkv_k_a[j], rwkv_r_k[j], rwkv_ln_w[j], rwkv_ln_b[j],
                            rwkv_w_o[j])
        g_final = norm_final_g if i == depth - 1 else None
        h = _convffn_layer(h, norm_ffn_g[i], ffn_w_in[i], ffn_conv_w[i], ffn_conv_b[i],
                           ffn_w_out[i], g_final)
    return h
```

```python
import functools
import math

import jax
import jax.numpy as jnp
from jax import lax
from jax.experimental import pallas as pl
from jax.experimental.pallas import tpu as pltpu

HEAD_DIM = 64
FNET_GROUPS = 4
RMS_EPS = 1e-6
GN_EPS = 64e-5
L2_EPS = 1e-12

LANES = 128
SUBLANES = 8
WKV_CHUNK = 64
VMEM_LIMIT_BYTES = 56 * 1024 * 1024

F32 = jnp.float32
BF16 = jnp.bfloat16


def _params(semantics):
    return pltpu.CompilerParams(dimension_semantics=semantics, vmem_limit_bytes=VMEM_LIMIT_BYTES)


def _rms(x, g):
    ms = jnp.mean(x * x, axis=-1, keepdims=True)
    return x * lax.rsqrt(ms + RMS_EPS) * g


def _dot(a, b):
    return jnp.dot(a, b, preferred_element_type=F32)


def _dot_nt(a, b):
    return lax.dot_general(a, b, (((1,), (1,)), ((), ())), preferred_element_type=F32)


def _dot_tn(a, b):
    return lax.dot_general(a, b, (((0,), (0,)), ((), ())), preferred_element_type=F32)


def _fnet_a_kernel(x_ref, g_ref, cs_ref, zc_ref, zs_ref):
    hn = _rms(x_ref[...], g_ref[...]).astype(BF16)
    gd = cs_ref.shape[0]
    for grp in range(hn.shape[1] // gd):
        sl = slice(grp * gd, (grp + 1) * gd)
        z = _dot(hn[:, sl], cs_ref[...])
        zc_ref[:, sl] = z[:, :gd].astype(BF16)
        zs_ref[:, sl] = z[:, gd:].astype(BF16)


def _fnet_b_kernel(c_ref, s_ref, zc_ref, zs_ref, wo_ref, bo_ref, x_ref, o_ref, *, scale):
    f = _dot(c_ref[...], zc_ref[...]) - _dot(s_ref[...], zs_ref[...])
    f = (f * scale).astype(BF16)
    o_ref[...] = x_ref[...] + _dot(f, wo_ref[...]) + bo_ref[...]


def _dft_mats(n):
    idx = lax.iota(jnp.int32, n)
    m = (idx[:, None] * idx[None, :]) % n
    ang = m.astype(F32) * F32(2.0 * math.pi / n)
    return jnp.cos(ang), jnp.sin(ang)


def _fnet_layer(x, g, w_o, b_o):
    b, s, d = x.shape
    gd = d // FNET_GROUPS
    t = b * s
    cc, sc = _dft_mats(gd)
    cs = jnp.concatenate([cc, sc], axis=1).astype(BF16)
    cs_seq, ss_seq = _dft_mats(s)
    cs_seq = cs_seq.astype(BF16)
    ss_seq = ss_seq.astype(BF16)
    tm = min(1024, t)
    zc, zs = pl.pallas_call(
        _fnet_a_kernel,
        grid=(t // tm,),
        in_specs=[pl.BlockSpec((tm, d), lambda i: (i, 0)),
                  pl.BlockSpec((1, d), lambda i: (0, 0)),
                  pl.BlockSpec((gd, 2 * gd), lambda i: (0, 0))],
        out_specs=[pl.BlockSpec((tm, d), lambda i: (i, 0)),
                   pl.BlockSpec((tm, d), lambda i: (i, 0))],
        out_shape=[jax.ShapeDtypeStruct((t, d), BF16), jax.ShapeDtypeStruct((t, d), BF16)],
        compiler_params=_params(("parallel",)),
        name="fnet_channel_dft",
    )(x.reshape(t, d), g.reshape(1, d), cs)
    zc = zc.reshape(b, s, d)
    zs = zs.reshape(b, s, d)
    ts = min(512, s)
    scale = 1.0 / math.sqrt(s * gd)
    return pl.pallas_call(
        functools.partial(_fnet_b_kernel, scale=scale),
        grid=(b, s // ts),
        in_specs=[pl.BlockSpec((ts, s), lambda bi, i: (i, 0)),
                  pl.BlockSpec((ts, s), lambda bi, i: (i, 0)),
                  pl.BlockSpec((None, s, d), lambda bi, i: (bi, 0, 0)),
                  pl.BlockSpec((None, s, d), lambda bi, i: (bi, 0, 0)),
                  pl.BlockSpec((d, d), lambda bi, i: (0, 0)),
                  pl.BlockSpec((1, d), lambda bi, i: (0, 0)),
                  pl.BlockSpec((None, ts, d), lambda bi, i: (bi, i, 0))],
        out_specs=pl.BlockSpec((None, ts, d), lambda bi, i: (bi, i, 0)),
        out_shape=jax.ShapeDtypeStruct((b, s, d), F32),
        compiler_params=_params(("parallel", "arbitrary")),
        name="fnet_seq_dft_proj",
    )(cs_seq, ss_seq, zc, zs, w_o.astype(BF16), b_o.reshape(1, d), x)


def _convffn_kernel(h_ref, g_ref, wg_ref, wv_ref, cwg_ref, cwv_ref, cbg_ref, cbv_ref, wo_ref,
                    gf_ref, o_ref, hn_ref, *, final_norm):
    j = pl.program_id(1)

    @pl.when(j == 0)
    def _():
        h = h_ref[...]
        hn_ref[...] = _rms(h, g_ref[...]).astype(BF16)
        o_ref[...] = h

    hn = hn_ref[...]
    s = hn.shape[0]
    row = lax.broadcasted_iota(jnp.int32, (s, 1), 0)
    first = row == 0
    last = row == s - 1

    def conv(u, cw_ref, cb_ref):
        up = jnp.where(first, 0.0, pltpu.roll(u, 1, 0))
        un = jnp.where(last, 0.0, pltpu.roll(u, s - 1, 0))
        cw = cw_ref[...]
        return up * cw[0:1] + u * cw[1:2] + un * cw[2:3] + cb_ref[...]

    ug = conv(_dot(hn, wg_ref[...]), cwg_ref, cbg_ref)
    uv = conv(_dot(hn, wv_ref[...]), cwv_ref, cbv_ref)
    act = (ug * jax.nn.sigmoid(ug) * uv).astype(BF16)
    o_ref[...] += _dot(act, wo_ref[...])

    if final_norm:
        @pl.when(j == pl.num_programs(1) - 1)
        def _():
            o_ref[...] = _rms(o_ref[...], gf_ref[...])


def _convffn_layer(h, g, w_in, conv_w, conv_b, w_out, g_final=None):
    b, s, d = h.shape
    dff = w_out.shape[0]
    ck = 256 if dff % 256 == 0 else dff
    nj = dff // ck
    w_in = w_in.astype(BF16)
    conv_b = conv_b.reshape(1, 2 * dff)
    final_norm = g_final is not None
    gf = (g_final if final_norm else g).reshape(1, d)
    return pl.pallas_call(
        functools.partial(_convffn_kernel, final_norm=final_norm),
        grid=(b, nj),
        in_specs=[pl.BlockSpec((None, s, d), lambda bi, j: (bi, 0, 0)),
                  pl.BlockSpec((1, d), lambda bi, j: (0, 0)),
                  pl.BlockSpec((d, ck), lambda bi, j: (0, j)),
                  pl.BlockSpec((d, ck), lambda bi, j: (0, nj + j)),
                  pl.BlockSpec((3, ck), lambda bi, j: (0, j)),
                  pl.BlockSpec((3, ck), lambda bi, j: (0, nj + j)),
                  pl.BlockSpec((1, ck), lambda bi, j: (0, j)),
                  pl.BlockSpec((1, ck), lambda bi, j: (0, nj + j)),
                  pl.BlockSpec((ck, d), lambda bi, j: (j, 0)),
                  pl.BlockSpec((1, d), lambda bi, j: (0, 0))],
        out_specs=pl.BlockSpec((None, s, d), lambda bi, j: (bi, 0, 0)),
        out_shape=jax.ShapeDtypeStruct((b, s, d), F32),
        scratch_shapes=[pltpu.VMEM((s, d), BF16)],
        compiler_params=_params(("parallel", "arbitrary")),
        name="convffn_final" if final_norm else "convffn",
    )(h, g.reshape(1, d), w_in, w_in, conv_w, conv_w, conv_b, conv_b, w_out.astype(BF16), gf)


def _rwkv_pre_kernel(h_ref, hp_ref, hx_ref, g_ref, mu_ref, wrkv_ref, w1_ref, w2_ref, w0_ref,
                     a1_ref, a2_ref, a0_ref, g1_ref, g2_ref, kk_ref, ka_ref, rk_ref, hsum_ref,
                     r_out, v_out, kkn_out, lw_out, kd_out, bd_out, bonus_out, gate_out):
    i = pl.program_id(1)
    g = g_ref[...]
    hn = _rms(h_ref[...], g)
    ts = hn.shape[0]
    prev_edge = _rms(hp_ref[...], g)[SUBLANES - 1:SUBLANES]
    next_edge = _rms(hx_ref[...], g)[0:1]
    prev_edge = jnp.where(i == 0, 0.0, prev_edge)
    next_edge = jnp.where(i == pl.num_programs(1) - 1, 0.0, next_edge)
    row = lax.broadcasted_iota(jnp.int32, (ts, 1), 0)
    prev = jnp.where(row == 0, prev_edge, pltpu.roll(hn, 1, 0))
    nxt = jnp.where(row == ts - 1, next_edge, pltpu.roll(hn, ts - 1, 0))
    dp = prev - hn
    dn = nxt - hn
    mu = mu_ref[...]
    n_mix = mu.shape[0] // 2

    def mix(m):
        return (hn + dp * mu[m:m + 1] + dn * mu[n_mix + m:n_mix + m + 1]).astype(BF16)

    r = _dot(mix(0), wrkv_ref[0])
    k = _dot(mix(1), wrkv_ref[1])
    v = _dot(mix(2), wrkv_ref[2])
    tw = jnp.tanh(_dot(mix(3), w1_ref[...])).astype(BF16)
    ta = _dot(mix(4), a1_ref[...]).astype(BF16)
    gate = _dot(jax.nn.sigmoid(_dot(mix(5), g1_ref[...])).astype(BF16), g2_ref[...])

    kraw = k * kk_ref[...]
    ssq = _dot((kraw * kraw).astype(BF16), hsum_ref[...])
    kkn = kraw * lax.rsqrt(jnp.maximum(ssq, L2_EPS * L2_EPS))
    ka = ka_ref[...]
    kd_sum = jnp.zeros_like(k)
    for zdir in range(2):
        xw = w0_ref[zdir:zdir + 1] + _dot(tw, w2_ref[zdir])
        w_log = -(jnp.maximum(-xw, 0.0) + jnp.log(1.0 + jnp.exp(-jnp.abs(xw)))) - 0.5
        lw_out[zdir] = -jnp.exp(w_log)
        ag = jax.nn.sigmoid(a0_ref[zdir:zdir + 1] + _dot(ta, a2_ref[zdir]))
        kd = k * (1.0 + (ag - 1.0) * ka)
        kd_sum = kd_sum + kd
        kd_out[zdir] = kd.astype(BF16)
        bd_out[zdir] = (kkn * ag).astype(BF16)
    bonus = _dot((r * kd_sum * rk_ref[...]).astype(BF16), hsum_ref[...]) * v
    r_out[...] = r.astype(BF16)
    v_out[...] = v.astype(BF16)
    kkn_out[...] = kkn.astype(BF16)
    bonus_out[...] = bonus
    gate_out[...] = gate


def _rwkv_pre(h, g, mu, w_rkv, w0, w1, w2, a0, a1, a2, g1, g2, k_k, k_a, r_k, hsum):
    b, s, d = h.shape
    ts = min(256, s)
    nt = s // ts
    nh8 = s // SUBLANES
    rdec = w1.shape[-1]
    raaa = a1.shape[-1]
    w1c = jnp.concatenate([w1[0], w1[1]], axis=1).astype(BF16)
    a1c = jnp.concatenate([a1[0], a1[1]], axis=1).astype(BF16)
    zw = jnp.zeros_like(w2[0])
    za = jnp.zeros_like(a2[0])
    w2p = jnp.stack([jnp.concatenate([w2[0], zw], 0), jnp.concatenate([zw, w2[1]], 0)]).astype(BF16)
    a2p = jnp.stack([jnp.concatenate([a2[0], za], 0), jnp.concatenate([za, a2[1]], 0)]).astype(BF16)
    full2 = lambda shape: pl.BlockSpec(shape, lambda bi, i: (0,) * len(shape))
    tile = pl.BlockSpec((None, ts, d), lambda bi, i: (bi, i, 0))
    tile2 = pl.BlockSpec((2, None, ts, d), lambda bi, i: (0, bi, i, 0))
    tpb = ts // SUBLANES
    in_specs = [
        tile,
        pl.BlockSpec((None, SUBLANES, d), lambda bi, i: (bi, jnp.maximum(i * tpb - 1, 0), 0)),
        pl.BlockSpec((None, SUBLANES, d), lambda bi, i: (bi, jnp.minimum((i + 1) * tpb, nh8 - 1), 0)),
        full2((1, d)), full2((2 * mu.shape[1], d)), full2((3, d, d)),
        full2((d, 2 * rdec)), full2((2, 2 * rdec, d)), full2((2, d)),
        full2((d, 2 * raaa)), full2((2, 2 * raaa, d)), full2((2, d)),
        full2(g1.shape), full2(g2.shape), full2((1, d)), full2((1, d)), full2((1, d)), full2((d, d)),
    ]
    out_specs = [tile, tile, tile, tile2, tile2, tile2, tile, tile]
    bsd = (b, s, d)
    out_shape = [jax.ShapeDtypeStruct(bsd, BF16), jax.ShapeDtypeStruct(bsd, BF16),
                 jax.ShapeDtypeStruct(bsd, BF16), jax.ShapeDtypeStruct((2,) + bsd, F32),
                 jax.ShapeDtypeStruct((2,) + bsd, BF16), jax.ShapeDtypeStruct((2,) + bsd, BF16),
                 jax.ShapeDtypeStruct(bsd, F32), jax.ShapeDtypeStruct(bsd, F32)]
    return pl.pallas_call(
        _rwkv_pre_kernel,
        grid=(b, nt),
        in_specs=in_specs,
        out_specs=out_specs,
        out_shape=out_shape,
        compiler_params=_params(("parallel", "arbitrary")),
        name="rwkv_projections",
    )(h, h, h, g.reshape(1, d), mu.reshape(2 * mu.shape[1], d), w_rkv.astype(BF16),
      w1c, w2p, w0, a1c, a2p, a0, g1.astype(BF16), g2.astype(BF16),
      k_k.reshape(1, d), k_a.reshape(1, d), r_k.reshape(1, d), hsum)


def _wkv_kernel(r_ref, v_ref, kk_ref, lw_ref, kd_ref, bd_ref, y_ref, st_ref):
    zdir = pl.program_id(1)
    c = pl.program_id(2)
    ch = r_ref.shape[0]
    pr = 2 * ch
    n_pairs = r_ref.shape[1] // LANES
    fwd = zdir == 0

    @pl.when(c == 0)
    def _():
        st_ref[...] = jnp.zeros_like(st_ref)

    ri = lax.broadcasted_iota(jnp.int32, (pr, pr), 0)
    ci = lax.broadcasted_iota(jnp.int32, (pr, pr), 1)
    head_r = jnp.where(ri >= ch, 1, 0)
    head_c = jnp.where(ci >= ch, 1, 0)
    same_head = head_r == head_c
    tt = ri - ch * head_r
    jj = ci - ch * head_c
    sign = jnp.where(fwd, 1, -1)
    order = (tt - jj) * sign
    strict = same_head & (order > 0)
    incl = same_head & (order >= 0)
    eye = (ri == ci).astype(F32)
    r8 = lax.broadcasted_iota(jnp.int32, (ch, ch), 0)
    c8 = lax.broadcasted_iota(jnp.int32, (ch, ch), 1)
    tri = jnp.where((r8 - c8) * sign >= 0, 1.0, 0.0).astype(BF16)

    lw = lw_ref[...]
    lw_hi = lw.astype(BF16)
    lw_lo = (lw - lw_hi.astype(F32)).astype(BF16)
    cum = _dot(tri, lw_hi) + _dot(tri, lw_lo)
    tot = jnp.where(fwd, cum[ch - 1:ch], cum[0:1])
    e_in = jnp.exp(cum)
    e_ex = jnp.exp(cum - lw)
    e_neg = jnp.exp(-cum)
    g_tot = jnp.exp(tot)
    r_t = r_ref[...].astype(F32) * e_in
    a_t = -(kk_ref[...].astype(F32) * e_ex)
    k_t = kd_ref[...].astype(F32) * e_neg
    b_t = bd_ref[...].astype(F32) * e_neg
    k_h = k_t * g_tot
    b_h = b_t * g_tot
    v_all = v_ref[...]

    head0 = lax.broadcasted_iota(jnp.int32, (1, LANES), 1) < HEAD_DIM

    def blockdiag(x):
        return jnp.concatenate([jnp.where(head0, x, 0.0), jnp.where(head0, 0.0, x)], axis=0).astype(BF16)

    n_levels = int(math.log2(ch))
    pairs = range(n_pairs)
    sls = [slice(p * LANES, (p + 1) * LANES) for p in pairs]
    lr = [jnp.concatenate([blockdiag(a_t[:, sl]), blockdiag(r_t[:, sl])], axis=0) for sl in sls]
    bk = [jnp.concatenate([blockdiag(b_t[:, sl]), blockdiag(k_t[:, sl])], axis=0) for sl in sls]
    bkh = [jnp.concatenate([blockdiag(b_h[:, sl]), blockdiag(k_h[:, sl])], axis=0) for sl in sls]
    vb = [blockdiag(v_all[:, sl].astype(F32)) for sl in sls]
    sc = [_dot_nt(lr[p], bk[p]) for p in pairs]
    lab = [jnp.where(strict, sc[p][:pr, :pr], 0.0) for p in pairs]
    lak = [jnp.where(strict, sc[p][:pr, pr:], 0.0).astype(BF16) for p in pairs]
    mr = [jnp.concatenate([jnp.where(incl, sc[p][pr:, :pr], 0.0).astype(BF16),
                           jnp.where(incl, sc[p][pr:, pr:], 0.0).astype(BF16)], axis=1) for p in pairs]
    t_inv = [eye + lab[p] for p in pairs]
    lab_b = [lab[p].astype(BF16) for p in pairs]
    pk = [_dot(lab_b[p], lab_b[p]) for p in pairs]
    for lvl in range(1, n_levels):
        pb = [pk[p].astype(BF16) for p in pairs]
        if lvl < n_levels - 1:
            tp = [_dot(jnp.concatenate([t_inv[p].astype(BF16), pb[p]], axis=0), pb[p]) for p in pairs]
            t_inv = [t_inv[p] + tp[p][:pr] for p in pairs]
            pk = [tp[p][pr:] for p in pairs]
        else:
            t_inv = [t_inv[p] + _dot(t_inv[p].astype(BF16), pb[p]) for p in pairs]
    lv = [_dot(lak[p], vb[p]) for p in pairs]
    st = [st_ref[p] for p in pairs]
    lrs = [_dot_nt(lr[p], st[p].astype(BF16)) for p in pairs]
    u = [_dot(t_inv[p].astype(BF16), (lrs[p][:pr] + lv[p]).astype(BF16)) for p in pairs]
    uv = [jnp.concatenate([u[p].astype(BF16), vb[p]], axis=0) for p in pairs]
    y = [lrs[p][pr:] + _dot(mr[p], uv[p]) for p in pairs]
    for p in pairs:
        y_ref[:, sls[p]] = y[p][:ch] + y[p][ch:]
    for p in pairs:
        st_ref[p] = st[p] * g_tot[:, sls[p]] + _dot_tn(uv[p], bkh[p])


def _wkv(r, v, kkn, lw, kd, bd):
    b, s, d = r.shape
    ch = min(WKV_CHUNK, s)
    nc = s // ch
    n_pairs = d // LANES

    def cidx(zdir, c):
        return jnp.where(zdir == 0, c, nc - 1 - c)

    shared = pl.BlockSpec((None, ch, d), lambda bi, zdir, c: (bi, cidx(zdir, c), 0))
    perdir = pl.BlockSpec((None, None, ch, d), lambda bi, zdir, c: (zdir, bi, cidx(zdir, c), 0))
    return pl.pallas_call(
        _wkv_kernel,
        grid=(b, 2, nc),
        in_specs=[shared, shared, shared, perdir, perdir, perdir],
        out_specs=perdir,
        out_shape=jax.ShapeDtypeStruct((2, b, s, d), F32),
        scratch_shapes=[pltpu.VMEM((n_pairs, LANES, LANES), F32)],
        compiler_params=_params(("parallel", "parallel", "arbitrary")),
        name="wkv7_chunked_scan",
    )(r, v, kkn, lw, kd, bd)


def _rwkv_post_kernel(y_ref, bonus_ref, gate_ref, h_ref, lnw_ref, lnb_ref, hmean_ref, wo_ref, o_ref):
    y = y_ref[0] + y_ref[1]
    mean = _dot(y.astype(BF16), hmean_ref[...])
    dlt = y - mean
    var = _dot((dlt * dlt).astype(BF16), hmean_ref[...])
    yn = dlt * lax.rsqrt(var + GN_EPS) * lnw_ref[...] + lnb_ref[...]
    out = ((yn + bonus_ref[...]) * gate_ref[...]).astype(BF16)
    o_ref[...] = h_ref[...] + _dot(out, wo_ref[...])


def _rwkv_post(y, bonus, gate, h, ln_w, ln_b, hmean, w_o):
    b, s, d = h.shape
    ts = min(512, s)
    tile = pl.BlockSpec((None, ts, d), lambda bi, i: (bi, i, 0))
    vec = pl.BlockSpec((1, d), lambda bi, i: (0, 0))
    mat = pl.BlockSpec((d, d), lambda bi, i: (0, 0))
    return pl.pallas_call(
        _rwkv_post_kernel,
        grid=(b, s // ts),
        in_specs=[pl.BlockSpec((2, None, ts, d), lambda bi, i: (0, bi, i, 0)),
                  tile, tile, tile, vec, vec, mat, mat],
        out_specs=tile,
        out_shape=jax.ShapeDtypeStruct((b, s, d), F32),
        compiler_params=_params(("parallel", "arbitrary")),
        name="rwkv_groupnorm_proj",
    )(y, bonus, gate, h, ln_w.reshape(1, d), ln_b.reshape(1, d), hmean, w_o.astype(BF16))


def _rwkv_layer(h, g, mu, w_rkv, w0, w1, w2, a0, a1, a2, g1, g2, k_k, k_a, r_k, ln_w, ln_b, w_o):
    d = h.shape[-1]
    head_of = lax.iota(jnp.int32, d) // HEAD_DIM
    hsum = (head_of[:, None] == head_of[None, :]).astype(BF16)
    hmean = (hsum.astype(F32) / HEAD_DIM).astype(BF16)
    r, v, kkn, lw, kd, bd, bonus, gate = _rwkv_pre(
        h, g, mu, w_rkv, w0, w1, w2, a0, a1, a2, g1, g2, k_k, k_a, r_k, hsum)
    y = _wkv(r, v, kkn, lw, kd, bd)
    return _rwkv_post(y, bonus, gate, h, ln_w, ln_b, hmean, w_o)


def kernel(x, norm_mix_g, norm_ffn_g, norm_final_g, fnet_w_o, fnet_b_o, rwkv_mu, rwkv_w_rkv, rwkv_w0, rwkv_w1, rwkv_w2, rwkv_a0, rwkv_a1, rwkv_a2, rwkv_g1, rwkv_g2, rwkv_k_k, rwkv_k_a, rwkv_r_k, rwkv_ln_w, rwkv_ln_b, rwkv_w_o, ffn_w_in, ffn_conv_w, ffn_conv_b, ffn_w_out):
    depth = norm_mix_g.shape[0]
    h = x
    for i in range(depth):
        j = i // 2
        if i % 2 == 0:
            h = _fnet_layer(h, norm_mix_g[i], fnet_w_o[j], fnet_b_o[j])
        else:
            h = _rwkv_layer(h, norm_mix_g[i], rwkv_mu[j], rwkv_w_rkv[j], rwkv_w0[j], rwkv_w1[j],
                            rwkv_w2[j], rwkv_a0[j], rwkv_a1[j], rwkv_a2[j], rwkv_g1[j], rwkv_g2[j],
                            rwkv_k_k[j], rwkv_k_a[j], rwkv_r_k[j], rwkv_ln_w[j], rwkv_ln_b[j],
                            rwkv_w_o[j])
        g_final = norm_final_g if i == depth - 1 else None
        h = _convffn_layer(h, norm_ffn_g[i], ffn_w_in[i], ffn_conv_w[i], ffn_conv_b[i],
                           ffn_w_out[i], g_final)
    return h
```

```python
import functools
import math

import jax
import jax.numpy as jnp
import numpy as np
from jax import lax
from jax.experimental import pallas as pl
from jax.experimental.pallas import tpu as pltpu

HEAD_DIM = 64
FNET_GROUPS = 4
RMS_EPS = 1e-6
GN_EPS = 64e-5
L2_EPS = 1e-12

LANES = 128
SUBLANES = 8
WKV_CHUNK = 64
WKV_CHUNKS_PER_STEP = 2
FFN_ROW_BLOCK = 512
VMEM_LIMIT_BYTES = 56 * 1024 * 1024

F32 = jnp.float32
BF16 = jnp.bfloat16


def _params(semantics):
    return pltpu.CompilerParams(dimension_semantics=semantics, vmem_limit_bytes=VMEM_LIMIT_BYTES)


def _rms(x, g):
    ms = jnp.mean(x * x, axis=-1, keepdims=True)
    return x * lax.rsqrt(ms + RMS_EPS) * g


def _dot(a, b):
    return jnp.dot(a, b, preferred_element_type=F32)


def _dot_nt(a, b):
    return lax.dot_general(a, b, (((1,), (1,)), ((), ())), preferred_element_type=F32)


def _dot_tn(a, b):
    return lax.dot_general(a, b, (((0,), (0,)), ((), ())), preferred_element_type=F32)


def _head_reduce(x, hs_ref, he_ref):
    part = _dot(x.astype(BF16), hs_ref[...])
    return _dot(part.astype(BF16), he_ref[...])


def _fnet_a_kernel(x_ref, g_ref, cs_ref, zc_ref, zs_ref):
    hn = _rms(x_ref[...], g_ref[...]).astype(BF16)
    gd = cs_ref.shape[0]
    for grp in range(hn.shape[1] // gd):
        sl = slice(grp * gd, (grp + 1) * gd)
        z = _dot(hn[:, sl], cs_ref[...])
        zc_ref[:, sl] = z[:, :gd].astype(BF16)
        zs_ref[:, sl] = z[:, gd:].astype(BF16)


def _fnet_b_kernel(c_ref, s_ref, zc_ref, zs_ref, wo_ref, bo_ref, x_ref, o_ref, *, scale):
    f = _dot(c_ref[...], zc_ref[...]) - _dot(s_ref[...], zs_ref[...])
    f = (f * scale).astype(BF16)
    o_ref[...] = x_ref[...] + _dot(f, wo_ref[...]) + bo_ref[...]


def _dft_mats(n):
    idx = np.arange(n)
    ang = (2.0 * np.pi / n) * ((idx[:, None] * idx[None, :]) % n)
    return jnp.asarray(np.cos(ang), F32).astype(BF16), jnp.asarray(np.sin(ang), F32).astype(BF16)


def _fnet_layer(x, g, w_o, b_o):
    b, s, d = x.shape
    gd = d // FNET_GROUPS
    t = b * s
    cc, sc = _dft_mats(gd)
    cs = jnp.concatenate([cc, sc], axis=1)
    cs_seq, ss_seq = _dft_mats(s)
    tm = min(1024, t)
    zc, zs = pl.pallas_call(
        _fnet_a_kernel,
        grid=(t // tm,),
        in_specs=[pl.BlockSpec((tm, d), lambda i: (i, 0)),
                  pl.BlockSpec((1, d), lambda i: (0, 0)),
                  pl.BlockSpec((gd, 2 * gd), lambda i: (0, 0))],
        out_specs=[pl.BlockSpec((tm, d), lambda i: (i, 0)),
                   pl.BlockSpec((tm, d), lambda i: (i, 0))],
        out_shape=[jax.ShapeDtypeStruct((t, d), BF16), jax.ShapeDtypeStruct((t, d), BF16)],
        compiler_params=_params(("parallel",)),
        name="fnet_channel_dft",
    )(x.reshape(t, d), g.reshape(1, d), cs)
    zc = zc.reshape(b, s, d)
    zs = zs.reshape(b, s, d)
    ts = min(512, s)
    scale = 1.0 / math.sqrt(s * gd)
    return pl.pallas_call(
        functools.partial(_fnet_b_kernel, scale=scale),
        grid=(b, s // ts),
        in_specs=[pl.BlockSpec((ts, s), lambda bi, i: (i, 0)),
                  pl.BlockSpec((ts, s), lambda bi, i: (i, 0)),
                  pl.BlockSpec((None, s, d), lambda bi, i: (bi, 0, 0)),
                  pl.BlockSpec((None, s, d), lambda bi, i: (bi, 0, 0)),
                  pl.BlockSpec((d, d), lambda bi, i: (0, 0)),
                  pl.BlockSpec((1, d), lambda bi, i: (0, 0)),
                  pl.BlockSpec((None, ts, d), lambda bi, i: (bi, i, 0))],
        out_specs=pl.BlockSpec((None, ts, d), lambda bi, i: (bi, i, 0)),
        out_shape=jax.ShapeDtypeStruct((b, s, d), F32),
        compiler_params=_params(("parallel", "arbitrary")),
        name="fnet_seq_dft_proj",
    )(cs_seq, ss_seq, zc, zs, w_o.astype(BF16), b_o.reshape(1, d), x)


def _ffn_act_kernel(h_ref, g_ref, wg_ref, wv_ref, cwg_ref, cwv_ref, cbg_ref, cbv_ref, act_ref,
                    hn_ref, ug_ref, uv_ref):
    @pl.when(pl.program_id(1) == 0)
    def _():
        hn_ref[...] = _rms(h_ref[...], g_ref[...]).astype(BF16)
        ug_ref[...] = jnp.zeros_like(ug_ref)
        uv_ref[...] = jnp.zeros_like(uv_ref)

    s, ck = ug_ref.shape
    rb = min(FFN_ROW_BLOCK, s)
    nb = s // rb
    zero_row = jnp.zeros((1, ck), F32)

    def edges(u_ref):
        prev = [zero_row if b == 0 else u_ref[pl.ds(b * rb - SUBLANES, SUBLANES), :][SUBLANES - 1:] for b in range(nb)]
        nxt = [zero_row if b == nb - 1 else u_ref[pl.ds((b + 1) * rb, SUBLANES), :][:1] for b in range(nb)]
        return prev, nxt

    edge_g = edges(ug_ref)
    edge_v = edges(uv_ref)
    row = lax.broadcasted_iota(jnp.int32, (rb, 1), 0)
    first = row == 0
    last = row == rb - 1
    for b in range(nb):
        rows = pl.ds(b * rb, rb)

        def conv(u_ref, edge, cw_ref, cb_ref):
            u = u_ref[rows, :]
            up = jnp.where(first, edge[0][b], pltpu.roll(u, 1, 0))
            un = jnp.where(last, edge[1][b], pltpu.roll(u, rb - 1, 0))
            cw = cw_ref[...]
            return up * cw[0:1] + u * cw[1:2] + un * cw[2:3] + cb_ref[...]

        ug = conv(ug_ref, edge_g, cwg_ref, cbg_ref)
        uv = conv(uv_ref, edge_v, cwv_ref, cbv_ref)
        act_ref[rows, :] = (ug * jax.nn.sigmoid(ug) * uv).astype(BF16)
        hb = hn_ref[rows, :]
        ug_ref[rows, :] = _dot(hb, wg_ref[...])
        uv_ref[rows, :] = _dot(hb, wv_ref[...])


def _ffn_out_kernel(act_ref, wo_ref, h_ref, gf_ref, o_ref, *, final_norm):
    o = h_ref[...] + _dot(act_ref[...], wo_ref[...])
    if final_norm:
        o = _rms(o, gf_ref[...])
    o_ref[...] = o


def _convffn_layer(h, g, w_in, conv_w, conv_b, w_out, g_final=None):
    b, s, d = h.shape
    dff = w_out.shape[0]
    ck = 256 if dff % 256 == 0 else dff
    nj = dff // ck
    w_in = w_in.astype(BF16)
    conv_b = conv_b.reshape(1, 2 * dff)
    mm = lambda j: jnp.minimum(j, nj - 1)
    ew = lambda j: jnp.maximum(j - 1, 0)
    act = pl.pallas_call(
        _ffn_act_kernel,
        grid=(b, nj + 1),
        in_specs=[pl.BlockSpec((None, s, d), lambda bi, j: (bi, 0, 0)),
                  pl.BlockSpec((1, d), lambda bi, j: (0, 0)),
                  pl.BlockSpec((d, ck), lambda bi, j: (0, mm(j))),
                  pl.BlockSpec((d, ck), lambda bi, j: (0, nj + mm(j))),
                  pl.BlockSpec((3, ck), lambda bi, j: (0, ew(j))),
                  pl.BlockSpec((3, ck), lambda bi, j: (0, nj + ew(j))),
                  pl.BlockSpec((1, ck), lambda bi, j: (0, ew(j))),
                  pl.BlockSpec((1, ck), lambda bi, j: (0, nj + ew(j)))],
        out_specs=pl.BlockSpec((None, s, ck), lambda bi, j: (bi, 0, ew(j))),
        out_shape=jax.ShapeDtypeStruct((b, s, dff), BF16),
        scratch_shapes=[pltpu.VMEM((s, d), BF16), pltpu.VMEM((s, ck), F32), pltpu.VMEM((s, ck), F32)],
        compiler_params=_params(("parallel", "arbitrary")),
        name="convffn_act",
    )(h, g.reshape(1, d), w_in, w_in, conv_w, conv_w, conv_b, conv_b)
    final_norm = g_final is not None
    gf = (g_final if final_norm else g).reshape(1, d)
    t = b * s
    tm = min(1024, t)
    out = pl.pallas_call(
        functools.partial(_ffn_out_kernel, final_norm=final_norm),
        grid=(t // tm,),
        in_specs=[pl.BlockSpec((tm, dff), lambda i: (i, 0)),
                  pl.BlockSpec((dff, d), lambda i: (0, 0)),
                  pl.BlockSpec((tm, d), lambda i: (i, 0)),
                  pl.BlockSpec((1, d), lambda i: (0, 0))],
        out_specs=pl.BlockSpec((tm, d), lambda i: (i, 0)),
        out_shape=jax.ShapeDtypeStruct((t, d), F32),
        compiler_params=_params(("parallel",)),
        name="convffn_out_final" if final_norm else "convffn_out",
    )(act.reshape(t, dff), w_out.astype(BF16), h.reshape(t, d), gf)
    return out.reshape(b, s, d)


def _rwkv_pre_kernel(h_ref, hp_ref, hx_ref, g_ref, mu_ref, wrkv_ref, w1_ref, w2_ref, w0_ref,
                     a1_ref, a2_ref, a0_ref, g1_ref, g2_ref, kk_ref, ka_ref, rk_ref, hs_ref, he_ref,
                     r_out, v_out, kkn_out, lw_out, kd_out, bd_out, bonus_out, gate_out):
    i = pl.program_id(1)
    g = g_ref[...]
    hn = _rms(h_ref[...], g)
    ts = hn.shape[0]
    prev_edge = _rms(hp_ref[...], g)[SUBLANES - 1:SUBLANES]
    next_edge = _rms(hx_ref[...], g)[0:1]
    prev_edge = jnp.where(i == 0, 0.0, prev_edge)
    next_edge = jnp.where(i == pl.num_programs(1) - 1, 0.0, next_edge)
    row = lax.broadcasted_iota(jnp.int32, (ts, 1), 0)
    prev = jnp.where(row == 0, prev_edge, pltpu.roll(hn, 1, 0))
    nxt = jnp.where(row == ts - 1, next_edge, pltpu.roll(hn, ts - 1, 0))
    dp = prev - hn
    dn = nxt - hn
    mu = mu_ref[...]
    n_mix = mu.shape[0] // 2

    def mix(m):
        return (hn + dp * mu[m:m + 1] + dn * mu[n_mix + m:n_mix + m + 1]).astype(BF16)

    r = _dot(mix(0), wrkv_ref[0])
    k = _dot(mix(1), wrkv_ref[1])
    v = _dot(mix(2), wrkv_ref[2])
    tw = jnp.tanh(_dot(mix(3), w1_ref[...])).astype(BF16)
    ta = _dot(mix(4), a1_ref[...]).astype(BF16)
    gate = _dot(jax.nn.sigmoid(_dot(mix(5), g1_ref[...])).astype(BF16), g2_ref[...])

    kraw = k * kk_ref[...]
    ssq = _head_reduce(kraw * kraw, hs_ref, he_ref)
    kkn = kraw * lax.rsqrt(jnp.maximum(ssq, L2_EPS * L2_EPS))
    ka = ka_ref[...]
    kd_sum = jnp.zeros_like(k)
    for zdir in range(2):
        xw = w0_ref[zdir:zdir + 1] + _dot(tw, w2_ref[zdir])
        w_log = -(jnp.maximum(-xw, 0.0) + jnp.log(1.0 + jnp.exp(-jnp.abs(xw)))) - 0.5
        lw_out[zdir] = -jnp.exp(w_log)
        ag = jax.nn.sigmoid(a0_ref[zdir:zdir + 1] + _dot(ta, a2_ref[zdir]))
        kd = k * (1.0 + (ag - 1.0) * ka)
        kd_sum = kd_sum + kd
        kd_out[zdir] = kd.astype(BF16)
        bd_out[zdir] = (kkn * ag).astype(BF16)
    bonus = _head_reduce(r * kd_sum * rk_ref[...], hs_ref, he_ref) * v
    r_out[...] = r.astype(BF16)
    v_out[...] = v.astype(BF16)
    kkn_out[...] = kkn.astype(BF16)
    bonus_out[...] = bonus
    gate_out[...] = gate


def _rwkv_pre(h, g, mu, w_rkv, w0, w1, w2, a0, a1, a2, g1, g2, k_k, k_a, r_k, hs, he):
    b, s, d = h.shape
    ts = min(256, s)
    nt = s // ts
    nh8 = s // SUBLANES
    rdec = w1.shape[-1]
    raaa = a1.shape[-1]
    w1c = jnp.concatenate([w1[0], w1[1]], axis=1).astype(BF16)
    a1c = jnp.concatenate([a1[0], a1[1]], axis=1).astype(BF16)
    zw = jnp.zeros_like(w2[0])
    za = jnp.zeros_like(a2[0])
    w2p = jnp.stack([jnp.concatenate([w2[0], zw], 0), jnp.concatenate([zw, w2[1]], 0)]).astype(BF16)
    a2p = jnp.stack([jnp.concatenate([a2[0], za], 0), jnp.concatenate([za, a2[1]], 0)]).astype(BF16)
    full2 = lambda shape: pl.BlockSpec(shape, lambda bi, i: (0,) * len(shape))
    tile = pl.BlockSpec((None, ts, d), lambda bi, i: (bi, i, 0))
    tile2 = pl.BlockSpec((2, None, ts, d), lambda bi, i: (0, bi, i, 0))
    tpb = ts // SUBLANES
    in_specs = [
        tile,
        pl.BlockSpec((None, SUBLANES, d), lambda bi, i: (bi, jnp.maximum(i * tpb - 1, 0), 0)),
        pl.BlockSpec((None, SUBLANES, d), lambda bi, i: (bi, jnp.minimum((i + 1) * tpb, nh8 - 1), 0)),
        full2((1, d)), full2((2 * mu.shape[1], d)), full2((3, d, d)),
        full2((d, 2 * rdec)), full2((2, 2 * rdec, d)), full2((2, d)),
        full2((d, 2 * raaa)), full2((2, 2 * raaa, d)), full2((2, d)),
        full2(g1.shape), full2(g2.shape), full2((1, d)), full2((1, d)), full2((1, d)),
        full2(hs.shape), full2(he.shape),
    ]
    out_specs = [tile, tile, tile, tile2, tile2, tile2, tile, tile]
    bsd = (b, s, d)
    out_shape = [jax.ShapeDtypeStruct(bsd, BF16), jax.ShapeDtypeStruct(bsd, BF16),
                 jax.ShapeDtypeStruct(bsd, BF16), jax.ShapeDtypeStruct((2,) + bsd, F32),
                 jax.ShapeDtypeStruct((2,) + bsd, BF16), jax.ShapeDtypeStruct((2,) + bsd, BF16),
                 jax.ShapeDtypeStruct(bsd, F32), jax.ShapeDtypeStruct(bsd, F32)]
    return pl.pallas_call(
        _rwkv_pre_kernel,
        grid=(b, nt),
        in_specs=in_specs,
        out_specs=out_specs,
        out_shape=out_shape,
        compiler_params=_params(("parallel", "arbitrary")),
        name="rwkv_projections",
    )(h, h, h, g.reshape(1, d), mu.reshape(2 * mu.shape[1], d), w_rkv.astype(BF16),
      w1c, w2p, w0, a1c, a2p, a0, g1.astype(BF16), g2.astype(BF16),
      k_k.reshape(1, d), k_a.reshape(1, d), r_k.reshape(1, d), hs, he)


def _wkv_kernel(r_ref, v_ref, kk_ref, lw_ref, kd_ref, bd_ref, y_ref, st_ref, *, ch):
    zdir = pl.program_id(1)
    c = pl.program_id(2)
    pr = 2 * ch
    n_sub = r_ref.shape[0] // ch
    n_pairs = r_ref.shape[1] // LANES
    fwd = zdir == 0

    @pl.when(c == 0)
    def _():
        st_ref[...] = jnp.zeros_like(st_ref)

    ri = lax.broadcasted_iota(jnp.int32, (pr, pr), 0)
    ci = lax.broadcasted_iota(jnp.int32, (pr, pr), 1)
    head_r = jnp.where(ri >= ch, 1, 0)
    head_c = jnp.where(ci >= ch, 1, 0)
    same_head = head_r == head_c
    tt = ri - ch * head_r
    jj = ci - ch * head_c
    sign = jnp.where(fwd, 1, -1)
    order = (tt - jj) * sign
    strict = same_head & (order > 0)
    incl = same_head & (order >= 0)
    eye = (ri == ci).astype(F32)
    r8 = lax.broadcasted_iota(jnp.int32, (ch, ch), 0)
    c8 = lax.broadcasted_iota(jnp.int32, (ch, ch), 1)
    tri = jnp.where((r8 - c8) * sign >= 0, 1.0, 0.0).astype(BF16)
    head0 = lax.broadcasted_iota(jnp.int32, (1, LANES), 1) < HEAD_DIM

    def blockdiag(x):
        return jnp.concatenate([jnp.where(head0, x, 0.0), jnp.where(head0, 0.0, x)], axis=0).astype(BF16)

    pairs = range(n_pairs)
    sls = [slice(p * LANES, (p + 1) * LANES) for p in pairs]
    rows, g_tot, lr, bk, bkh, vb = [], [], [], [], [], []
    for k in range(n_sub):
        start = pl.multiple_of(jnp.where(fwd, k * ch, (n_sub - 1 - k) * ch), ch)
        rk = pl.ds(start, ch)
        rows.append(rk)
        lw = lw_ref[rk, :]
        lw_hi = lw.astype(BF16)
        lw_lo = (lw - lw_hi.astype(F32)).astype(BF16)
        cum = _dot(tri, lw_hi) + _dot(tri, lw_lo)
        tot = jnp.where(fwd, cum[ch - 1:ch], cum[0:1])
        e_neg = jnp.exp(-cum)
        gk = jnp.exp(tot)
        r_t = r_ref[rk, :].astype(F32) * jnp.exp(cum)
        a_t = -(kk_ref[rk, :].astype(F32) * jnp.exp(cum - lw))
        k_t = kd_ref[rk, :].astype(F32) * e_neg
        b_t = bd_ref[rk, :].astype(F32) * e_neg
        k_h = k_t * gk
        b_h = b_t * gk
        v_k = v_ref[rk, :].astype(F32)
        g_tot.append(gk)
        lr.append([jnp.concatenate([blockdiag(a_t[:, sl]), blockdiag(r_t[:, sl])], axis=0) for sl in sls])
        bk.append([jnp.concatenate([blockdiag(b_t[:, sl]), blockdiag(k_t[:, sl])], axis=0) for sl in sls])
        bkh.append([jnp.concatenate([blockdiag(b_h[:, sl]), blockdiag(k_h[:, sl])], axis=0) for sl in sls])
        vb.append([blockdiag(v_k[:, sl]) for sl in sls])

    chains = [(k, p) for k in range(n_sub) for p in pairs]
    n_levels = int(math.log2(ch))
    sc = {kp: _dot_nt(lr[kp[0]][kp[1]], bk[kp[0]][kp[1]]) for kp in chains}
    lab = {kp: jnp.where(strict, sc[kp][:pr, :pr], 0.0) for kp in chains}
    lak = {kp: jnp.where(strict, sc[kp][:pr, pr:], 0.0).astype(BF16) for kp in chains}
    mr = {kp: jnp.concatenate([jnp.where(incl, sc[kp][pr:, :pr], 0.0).astype(BF16),
                               jnp.where(incl, sc[kp][pr:, pr:], 0.0).astype(BF16)], axis=1) for kp in chains}
    t_inv = {kp: eye + lab[kp] for kp in chains}
    lab_b = {kp: lab[kp].astype(BF16) for kp in chains}
    pk = {kp: _dot(lab_b[kp], lab_b[kp]) for kp in chains}
    for lvl in range(1, n_levels):
        pb = {kp: pk[kp].astype(BF16) for kp in chains}
        if lvl < n_levels - 1:
            tp = {kp: _dot(jnp.concatenate([t_inv[kp].astype(BF16), pb[kp]], axis=0), pb[kp]) for kp in chains}
            t_inv = {kp: t_inv[kp] + tp[kp][:pr] for kp in chains}
            pk = {kp: tp[kp][pr:] for kp in chains}
        else:
            t_inv = {kp: t_inv[kp] + _dot(t_inv[kp].astype(BF16), pb[kp]) for kp in chains}
    t_b = {kp: t_inv[kp].astype(BF16) for kp in chains}
    lv = {kp: _dot(lak[kp], vb[kp[0]][kp[1]]) for kp in chains}

    st = [st_ref[p] for p in pairs]
    for k in range(n_sub):
        lrs = [_dot_nt(lr[k][p], st[p].astype(BF16)) for p in pairs]
        u = [_dot(t_b[(k, p)], (lrs[p][:pr] + lv[(k, p)]).astype(BF16)) for p in pairs]
        uv = [jnp.concatenate([u[p].astype(BF16), vb[k][p]], axis=0) for p in pairs]
        y = [lrs[p][pr:] + _dot(mr[(k, p)], uv[p]) for p in pairs]
        for p in pairs:
            y_ref[rows[k], sls[p]] = y[p][:ch] + y[p][ch:]
        st = [st[p] * g_tot[k][:, sls[p]] + _dot_tn(uv[p], bkh[k][p]) for p in pairs]
    for p in pairs:
        st_ref[p] = st[p]


def _wkv(r, v, kkn, lw, kd, bd):
    b, s, d = r.shape
    ch = min(WKV_CHUNK, s)
    rows = min(WKV_CHUNKS_PER_STEP * ch, s)
    nb = s // rows
    n_pairs = d // LANES

    def bidx(zdir, c):
        return jnp.where(zdir == 0, c, nb - 1 - c)

    shared = pl.BlockSpec((None, rows, d), lambda bi, zdir, c: (bi, bidx(zdir, c), 0))
    perdir = pl.BlockSpec((None, None, rows, d), lambda bi, zdir, c: (zdir, bi, bidx(zdir, c), 0))
    return pl.pallas_call(
        functools.partial(_wkv_kernel, ch=ch),
        grid=(b, 2, nb),
        in_specs=[shared, shared, shared, perdir, perdir, perdir],
        out_specs=perdir,
        out_shape=jax.ShapeDtypeStruct((2, b, s, d), F32),
        scratch_shapes=[pltpu.VMEM((n_pairs, LANES, LANES), F32)],
        compiler_params=_params(("parallel", "parallel", "arbitrary")),
        name="wkv7_chunked_scan",
    )(r, v, kkn, lw, kd, bd)


def _rwkv_post_kernel(y_ref, bonus_ref, gate_ref, h_ref, lnw_ref, lnb_ref, hs_ref, hem_ref, wo_ref, o_ref):
    y = y_ref[0] + y_ref[1]
    mean = _head_reduce(y, hs_ref, hem_ref)
    dlt = y - mean
    var = _head_reduce(dlt * dlt, hs_ref, hem_ref)
    yn = dlt * lax.rsqrt(var + GN_EPS) * lnw_ref[...] + lnb_ref[...]
    out = ((yn + bonus_ref[...]) * gate_ref[...]).astype(BF16)
    o_ref[...] = h_ref[...] + _dot(out, wo_ref[...])


def _rwkv_post(y, bonus, gate, h, ln_w, ln_b, hs, hem, w_o):
    b, s, d = h.shape
    ts = min(512, s)
    tile = pl.BlockSpec((None, ts, d), lambda bi, i: (bi, i, 0))
    vec = pl.BlockSpec((1, d), lambda bi, i: (0, 0))
    mat = pl.BlockSpec((d, d), lambda bi, i: (0, 0))
    return pl.pallas_call(
        _rwkv_post_kernel,
        grid=(b, s // ts),
        in_specs=[pl.BlockSpec((2, None, ts, d), lambda bi, i: (0, bi, i, 0)),
                  tile, tile, tile, vec, vec,
                  pl.BlockSpec(hs.shape, lambda bi, i: (0, 0)), pl.BlockSpec(hem.shape, lambda bi, i: (0, 0)), mat],
        out_specs=tile,
        out_shape=jax.ShapeDtypeStruct((b, s, d), F32),
        compiler_params=_params(("parallel", "arbitrary")),
        name="rwkv_groupnorm_proj",
    )(y, bonus, gate, h, ln_w.reshape(1, d), ln_b.reshape(1, d), hs, hem, w_o.astype(BF16))


def _rwkv_layer(h, g, mu, w_rkv, w0, w1, w2, a0, a1, a2, g1, g2, k_k, k_a, r_k, ln_w, ln_b, w_o):
    d = h.shape[-1]
    head_of = np.arange(d) // HEAD_DIM
    ind = (head_of[:, None] == np.arange(LANES)[None, :]).astype(np.float32)
    hs = jnp.asarray(ind).astype(BF16)
    he = jnp.asarray(ind.T).astype(BF16)
    hem = jnp.asarray(ind.T / HEAD_DIM).astype(BF16)
    r, v, kkn, lw, kd, bd, bonus, gate = _rwkv_pre(
        h, g, mu, w_rkv, w0, w1, w2, a0, a1, a2, g1, g2, k_k, k_a, r_k, hs, he)
    y = _wkv(r, v, kkn, lw, kd, bd)
    return _rwkv_post(y, bonus, gate, h, ln_w, ln_b, hs, hem, w_o)


def kernel(x, norm_mix_g, norm_ffn_g, norm_final_g, fnet_w_o, fnet_b_o, rwkv_mu, rwkv_w_rkv, rwkv_w0, rwkv_w1, rwkv_w2, rwkv_a0, rwkv_a1, rwkv_a2, rwkv_g1, rwkv_g2, rwkv_k_k, rwkv_k_a, rwkv_r_k, rwkv_ln_w, rwkv_ln_b, rwkv_w_o, ffn_w_in, ffn_conv_w, ffn_conv_b, ffn_w_out):
    depth = norm_mix_g.shape[0]
    h = x
    for i in range(depth):
        j = i // 2
        if i % 2 == 0:
            h = _fnet_layer(h, norm_mix_g[i], fnet_w_o[j], fnet_b_o[j])
        else:
            h = _rwkv_layer(h, norm_mix_g[i], rwkv_mu[j], rwkv_w_rkv[j], rwkv_w0[j], rwkv_w1[j],
                            rwkv_w2[j], rwkv_a0[j], rwkv_a1[j], rwkv_a2[j], rwkv_g1[j], rwkv_g2[j],
                            rwkv_k_k[j], rwkv_k_a[j], rwkv_r_k[j], rwkv_ln_w[j], rwkv_ln_b[j],
                            rwkv_w_o[j])
        g_final = norm_final_g if i == depth - 1 else None
        h = _convffn_layer(h, norm_ffn_g[i], ffn_w_in[i], ffn_conv_w[i], ffn_conv_b[i],
                           ffn_w_out[i], g_final)
    return h
```

```python
import functools
import math

import jax
import jax.numpy as jnp
import numpy as np
from jax import lax
from jax.experimental import pallas as pl
from jax.experimental.pallas import tpu as pltpu

HEAD_DIM = 64
FNET_GROUPS = 4
RMS_EPS = 1e-6
GN_EPS = 64e-5
L2_EPS = 1e-12

LANES = 128
SUBLANES = 8
WKV_CHUNK = 64
WKV_CHUNKS_PER_STEP = 4
FFN_ROW_BLOCK = 512
POST_SUB_BLOCKS = 2
PRE_SUB_BLOCKS = 2
PRE_ROWS = 512
VMEM_LIMIT_BYTES = 56 * 1024 * 1024

F32 = jnp.float32
BF16 = jnp.bfloat16


def _params(semantics):
    return pltpu.CompilerParams(dimension_semantics=semantics, vmem_limit_bytes=VMEM_LIMIT_BYTES)


def _rms(x, g):
    ms = jnp.mean(x * x, axis=-1, keepdims=True)
    return x * lax.rsqrt(ms + RMS_EPS) * g


def _dot(a, b):
    return jnp.dot(a, b, preferred_element_type=F32)


def _dot_nt(a, b):
    return lax.dot_general(a, b, (((1,), (1,)), ((), ())), preferred_element_type=F32)


def _dot_tn(a, b):
    return lax.dot_general(a, b, (((0,), (0,)), ((), ())), preferred_element_type=F32)


def _head_reduce(x, hs_ref, he_ref):
    part = _dot(x.astype(BF16), hs_ref[...])
    return _dot(part.astype(BF16), he_ref[...])


def _fnet_a_kernel(x_ref, g_ref, cs_ref, zc_ref, zs_ref):
    hn = _rms(x_ref[...], g_ref[...]).astype(BF16)
    gd = cs_ref.shape[0]
    for grp in range(hn.shape[1] // gd):
        sl = slice(grp * gd, (grp + 1) * gd)
        z = _dot(hn[:, sl], cs_ref[...])
        zc_ref[:, sl] = z[:, :gd].astype(BF16)
        zs_ref[:, sl] = z[:, gd:].astype(BF16)


def _fnet_b_kernel(c_ref, s_ref, zc_ref, zs_ref, wo_ref, bo_ref, x_ref, o_ref, *, scale):
    f = _dot(c_ref[...], zc_ref[...]) - _dot(s_ref[...], zs_ref[...])
    f = (f * scale).astype(BF16)
    o_ref[...] = x_ref[...] + _dot(f, wo_ref[...]) + bo_ref[...]


def _dft_mats(n):
    idx = np.arange(n)
    ang = (2.0 * np.pi / n) * ((idx[:, None] * idx[None, :]) % n)
    return jnp.asarray(np.cos(ang), F32).astype(BF16), jnp.asarray(np.sin(ang), F32).astype(BF16)


def _fnet_layer(x, g, w_o, b_o):
    b, s, d = x.shape
    gd = d // FNET_GROUPS
    t = b * s
    cc, sc = _dft_mats(gd)
    cs = jnp.concatenate([cc, sc], axis=1)
    cs_seq, ss_seq = _dft_mats(s)
    tm = min(1024, t)
    zc, zs = pl.pallas_call(
        _fnet_a_kernel,
        grid=(t // tm,),
        in_specs=[pl.BlockSpec((tm, d), lambda i: (i, 0)),
                  pl.BlockSpec((1, d), lambda i: (0, 0)),
                  pl.BlockSpec((gd, 2 * gd), lambda i: (0, 0))],
        out_specs=[pl.BlockSpec((tm, d), lambda i: (i, 0)),
                   pl.BlockSpec((tm, d), lambda i: (i, 0))],
        out_shape=[jax.ShapeDtypeStruct((t, d), BF16), jax.ShapeDtypeStruct((t, d), BF16)],
        compiler_params=_params(("parallel",)),
        name="fnet_channel_dft",
    )(x.reshape(t, d), g.reshape(1, d), cs)
    zc = zc.reshape(b, s, d)
    zs = zs.reshape(b, s, d)
    ts = min(512, s)
    scale = 1.0 / math.sqrt(s * gd)
    return pl.pallas_call(
        functools.partial(_fnet_b_kernel, scale=scale),
        grid=(b, s // ts),
        in_specs=[pl.BlockSpec((ts, s), lambda bi, i: (i, 0)),
                  pl.BlockSpec((ts, s), lambda bi, i: (i, 0)),
                  pl.BlockSpec((None, s, d), lambda bi, i: (bi, 0, 0)),
                  pl.BlockSpec((None, s, d), lambda bi, i: (bi, 0, 0)),
                  pl.BlockSpec((d, d), lambda bi, i: (0, 0)),
                  pl.BlockSpec((1, d), lambda bi, i: (0, 0)),
                  pl.BlockSpec((None, ts, d), lambda bi, i: (bi, i, 0))],
        out_specs=pl.BlockSpec((None, ts, d), lambda bi, i: (bi, i, 0)),
        out_shape=jax.ShapeDtypeStruct((b, s, d), F32),
        compiler_params=_params(("parallel", "arbitrary")),
        name="fnet_seq_dft_proj",
    )(cs_seq, ss_seq, zc, zs, w_o.astype(BF16), b_o.reshape(1, d), x)


def _ffn_act_kernel(h_ref, g_ref, wg_ref, wv_ref, cwg_ref, cwv_ref, cbg_ref, cbv_ref, act_ref,
                    hn_ref, ug_ref, uv_ref, *, row_block):
    @pl.when(pl.program_id(1) == 0)
    def _():
        hn_ref[...] = _rms(h_ref[...], g_ref[...]).astype(BF16)
        ug_ref[...] = jnp.zeros_like(ug_ref)
        uv_ref[...] = jnp.zeros_like(uv_ref)

    s, ck = ug_ref.shape
    rb = min(row_block, s)
    nb = s // rb
    zero_row = jnp.zeros((1, ck), F32)

    def edges(u_ref):
        prev = [zero_row if b == 0 else u_ref[pl.ds(b * rb - SUBLANES, SUBLANES), :][SUBLANES - 1:] for b in range(nb)]
        nxt = [zero_row if b == nb - 1 else u_ref[pl.ds((b + 1) * rb, SUBLANES), :][:1] for b in range(nb)]
        return prev, nxt

    edge_g = edges(ug_ref)
    edge_v = edges(uv_ref)
    row = lax.broadcasted_iota(jnp.int32, (rb, 1), 0)
    first = row == 0
    last = row == rb - 1
    for b in range(nb):
        rows = pl.ds(b * rb, rb)

        def conv(u_ref, edge, cw_ref, cb_ref):
            u = u_ref[rows, :]
            up = jnp.where(first, edge[0][b], pltpu.roll(u, 1, 0))
            un = jnp.where(last, edge[1][b], pltpu.roll(u, rb - 1, 0))
            cw = cw_ref[...]
            return up * cw[0:1] + u * cw[1:2] + un * cw[2:3] + cb_ref[...]

        ug = conv(ug_ref, edge_g, cwg_ref, cbg_ref)
        uv = conv(uv_ref, edge_v, cwv_ref, cbv_ref)
        act_ref[rows, :] = (ug * jax.nn.sigmoid(ug) * uv).astype(BF16)
        hb = hn_ref[rows, :]
        ug_ref[rows, :] = _dot(hb, wg_ref[...])
        uv_ref[rows, :] = _dot(hb, wv_ref[...])


def _ffn_out_kernel(act_ref, wo_ref, h_ref, gf_ref, o_ref, *, final_norm):
    o = h_ref[...] + _dot(act_ref[...], wo_ref[...])
    if final_norm:
        o = _rms(o, gf_ref[...])
    o_ref[...] = o


def _convffn_layer(h, g, w_in, conv_w, conv_b, w_out, g_final=None, row_block=FFN_ROW_BLOCK):
    b, s, d = h.shape
    dff = w_out.shape[0]
    ck = 256 if dff % 256 == 0 else dff
    nj = dff // ck
    w_in = w_in.astype(BF16)
    conv_b = conv_b.reshape(1, 2 * dff)
    mm = lambda j: jnp.minimum(j, nj - 1)
    ew = lambda j: jnp.maximum(j - 1, 0)
    act = pl.pallas_call(
        functools.partial(_ffn_act_kernel, row_block=row_block),
        grid=(b, nj + 1),
        in_specs=[pl.BlockSpec((None, s, d), lambda bi, j: (bi, 0, 0)),
                  pl.BlockSpec((1, d), lambda bi, j: (0, 0)),
                  pl.BlockSpec((d, ck), lambda bi, j: (0, mm(j))),
                  pl.BlockSpec((d, ck), lambda bi, j: (0, nj + mm(j))),
                  pl.BlockSpec((3, ck), lambda bi, j: (0, ew(j))),
                  pl.BlockSpec((3, ck), lambda bi, j: (0, nj + ew(j))),
                  pl.BlockSpec((1, ck), lambda bi, j: (0, ew(j))),
                  pl.BlockSpec((1, ck), lambda bi, j: (0, nj + ew(j)))],
        out_specs=pl.BlockSpec((None, s, ck), lambda bi, j: (bi, 0, ew(j))),
        out_shape=jax.ShapeDtypeStruct((b, s, dff), BF16),
        scratch_shapes=[pltpu.VMEM((s, d), BF16), pltpu.VMEM((s, ck), F32), pltpu.VMEM((s, ck), F32)],
        compiler_params=_params(("parallel", "arbitrary")),
        name="convffn_act",
    )(h, g.reshape(1, d), w_in, w_in, conv_w, conv_w, conv_b, conv_b)
    final_norm = g_final is not None
    gf = (g_final if final_norm else g).reshape(1, d)
    t = b * s
    tm = min(1024, t)
    out = pl.pallas_call(
        functools.partial(_ffn_out_kernel, final_norm=final_norm),
        grid=(t // tm,),
        in_specs=[pl.BlockSpec((tm, dff), lambda i: (i, 0)),
                  pl.BlockSpec((dff, d), lambda i: (0, 0)),
                  pl.BlockSpec((tm, d), lambda i: (i, 0)),
                  pl.BlockSpec((1, d), lambda i: (0, 0))],
        out_specs=pl.BlockSpec((tm, d), lambda i: (i, 0)),
        out_shape=jax.ShapeDtypeStruct((t, d), F32),
        compiler_params=_params(("parallel",)),
        name="convffn_out_final" if final_norm else "convffn_out",
    )(act.reshape(t, dff), w_out.astype(BF16), h.reshape(t, d), gf)
    return out.reshape(b, s, d)


def _rwkv_pre_kernel(h_ref, hp_ref, hx_ref, g_ref, mu_ref, wrkv_ref, w1_ref, w2_ref, w0_ref,
                     a1_ref, a2_ref, a0_ref, g1_ref, g2_ref, kk_ref, ka_ref, rk_ref, hs_ref, he_ref,
                     r_out, v_out, kkn_out, lw_out, kd_out, bd_out, bonus_out, gate_out):
    i = pl.program_id(1)
    g = g_ref[...]
    hn = _rms(h_ref[...], g)
    ts = hn.shape[0]
    prev_edge = _rms(hp_ref[...], g)[SUBLANES - 1:SUBLANES]
    next_edge = _rms(hx_ref[...], g)[0:1]
    prev_edge = jnp.where(i == 0, 0.0, prev_edge)
    next_edge = jnp.where(i == pl.num_programs(1) - 1, 0.0, next_edge)
    row = lax.broadcasted_iota(jnp.int32, (ts, 1), 0)
    prev = jnp.where(row == 0, prev_edge, pltpu.roll(hn, 1, 0))
    nxt = jnp.where(row == ts - 1, next_edge, pltpu.roll(hn, ts - 1, 0))
    dp = prev - hn
    dn = nxt - hn
    mu = mu_ref[...]
    n_mix = mu.shape[0] // 2

    rb = ts // PRE_SUB_BLOCKS if ts % PRE_SUB_BLOCKS == 0 else ts
    subs = [slice(j * rb, (j + 1) * rb) for j in range(ts // rb)]

    def mix(m):
        return [(hn[sb] + dp[sb] * mu[m:m + 1] + dn[sb] * mu[n_mix + m:n_mix + m + 1]).astype(BF16) for sb in subs]

    r = [_dot(x, wrkv_ref[0]) for x in mix(0)]
    k = [_dot(x, wrkv_ref[1]) for x in mix(1)]
    v = [_dot(x, wrkv_ref[2]) for x in mix(2)]
    tw = [jnp.tanh(_dot(x, w1_ref[...])).astype(BF16) for x in mix(3)]
    ta = [_dot(x, a1_ref[...]).astype(BF16) for x in mix(4)]
    tg = [jax.nn.sigmoid(_dot(x, g1_ref[...])).astype(BF16) for x in mix(5)]
    gate = [_dot(x, g2_ref[...]) for x in tg]

    kraw = [x * kk_ref[...] for x in k]
    ssq = [_head_reduce(x * x, hs_ref, he_ref) for x in kraw]
    kkn = [x * lax.rsqrt(jnp.maximum(q, L2_EPS * L2_EPS)) for x, q in zip(kraw, ssq)]
    kka = [x * ka_ref[...] for x in k]
    kd_base = [x - y for x, y in zip(k, kka)]
    kd_sum = [jnp.zeros_like(x) for x in k]
    for zdir in range(2):
        xw = [w0_ref[zdir:zdir + 1] + _dot(x, w2_ref[zdir]) for x in tw]
        la = [_dot(x, a2_ref[zdir]) for x in ta]
        for j, sb in enumerate(subs):
            lw_out[zdir, sb, :] = jax.nn.sigmoid(xw[j]) * (-math.exp(-0.5))
            ag = jax.nn.sigmoid(a0_ref[zdir:zdir + 1] + la[j])
            kd = kd_base[j] + kka[j] * ag
            kd_sum[j] = kd_sum[j] + kd
            kd_out[zdir, sb, :] = kd.astype(BF16)
            bd_out[zdir, sb, :] = (kkn[j] * ag).astype(BF16)
    bsum = [_head_reduce(r[j] * kd_sum[j] * rk_ref[...], hs_ref, he_ref) for j in range(len(subs))]
    for j, sb in enumerate(subs):
        r_out[sb, :] = r[j].astype(BF16)
        v_out[sb, :] = v[j].astype(BF16)
        kkn_out[sb, :] = kkn[j].astype(BF16)
        bonus_out[sb, :] = (bsum[j] * v[j]).astype(bonus_out.dtype)
        gate_out[sb, :] = gate[j].astype(gate_out.dtype)


def _rwkv_pre(h, g, mu, w_rkv, w0, w1, w2, a0, a1, a2, g1, g2, k_k, k_a, r_k, hs, he):
    b, s, d = h.shape
    ts = min(PRE_ROWS, s)
    nt = s // ts
    nh8 = s // SUBLANES
    rdec = w1.shape[-1]
    raaa = a1.shape[-1]
    w1c = jnp.concatenate([w1[0], w1[1]], axis=1).astype(BF16)
    a1c = jnp.concatenate([a1[0], a1[1]], axis=1).astype(BF16)
    zw = jnp.zeros_like(w2[0])
    za = jnp.zeros_like(a2[0])
    w2p = jnp.stack([jnp.concatenate([w2[0], zw], 0), jnp.concatenate([zw, w2[1]], 0)]).astype(BF16)
    a2p = jnp.stack([jnp.concatenate([a2[0], za], 0), jnp.concatenate([za, a2[1]], 0)]).astype(BF16)
    full2 = lambda shape: pl.BlockSpec(shape, lambda bi, i: (0,) * len(shape), pipeline_mode=pl.Buffered(1))
    tile = pl.BlockSpec((None, ts, d), lambda bi, i: (bi, i, 0))
    tile2 = pl.BlockSpec((2, None, ts, d), lambda bi, i: (0, bi, i, 0))
    tpb = ts // SUBLANES
    in_specs = [
        tile,
        pl.BlockSpec((None, SUBLANES, d), lambda bi, i: (bi, jnp.maximum(i * tpb - 1, 0), 0)),
        pl.BlockSpec((None, SUBLANES, d), lambda bi, i: (bi, jnp.minimum((i + 1) * tpb, nh8 - 1), 0)),
        full2((1, d)), full2((2 * mu.shape[1], d)), full2((3, d, d)),
        full2((d, 2 * rdec)), full2((2, 2 * rdec, d)), full2((2, d)),
        full2((d, 2 * raaa)), full2((2, 2 * raaa, d)), full2((2, d)),
        full2(g1.shape), full2(g2.shape), full2((1, d)), full2((1, d)), full2((1, d)),
        full2(hs.shape), full2(he.shape),
    ]
    out_specs = [tile, tile, tile, tile2, tile2, tile2, tile, tile]
    bsd = (b, s, d)
    out_shape = [jax.ShapeDtypeStruct(bsd, BF16), jax.ShapeDtypeStruct(bsd, BF16),
                 jax.ShapeDtypeStruct(bsd, BF16), jax.ShapeDtypeStruct((2,) + bsd, F32),
                 jax.ShapeDtypeStruct((2,) + bsd, BF16), jax.ShapeDtypeStruct((2,) + bsd, BF16),
                 jax.ShapeDtypeStruct(bsd, BF16), jax.ShapeDtypeStruct(bsd, BF16)]
    return pl.pallas_call(
        _rwkv_pre_kernel,
        grid=(b, nt),
        in_specs=in_specs,
        out_specs=out_specs,
        out_shape=out_shape,
        compiler_params=_params(("parallel", "arbitrary")),
        name="rwkv_projections",
    )(h, h, h, g.reshape(1, d), mu.reshape(2 * mu.shape[1], d), w_rkv.astype(BF16),
      w1c, w2p, w0, a1c, a2p, a0, g1.astype(BF16), g2.astype(BF16),
      k_k.reshape(1, d), k_a.reshape(1, d), r_k.reshape(1, d), hs, he)


def _wkv_kernel(r_ref, v_ref, kk_ref, lw_ref, kd_ref, bd_ref, y_ref, st_ref, *, ch):
    zdir = pl.program_id(1)
    c = pl.program_id(2)
    pr = 2 * ch
    n_sub = r_ref.shape[0] // ch
    n_pairs = r_ref.shape[1] // LANES
    fwd = zdir == 0

    @pl.when(c == 0)
    def _():
        st_ref[...] = jnp.zeros_like(st_ref)

    ri = lax.broadcasted_iota(jnp.int32, (pr, pr), 0)
    ci = lax.broadcasted_iota(jnp.int32, (pr, pr), 1)
    head_r = jnp.where(ri >= ch, 1, 0)
    head_c = jnp.where(ci >= ch, 1, 0)
    same_head = head_r == head_c
    tt = ri - ch * head_r
    jj = ci - ch * head_c
    sign = jnp.where(fwd, 1, -1)
    order = (tt - jj) * sign
    strict = same_head & (order > 0)
    incl = same_head & (order >= 0)
    eye = (ri == ci).astype(F32)
    r8 = lax.broadcasted_iota(jnp.int32, (ch, ch), 0)
    c8 = lax.broadcasted_iota(jnp.int32, (ch, ch), 1)
    tri = jnp.where((r8 - c8) * sign >= 0, 1.0, 0.0).astype(BF16)
    head0 = lax.broadcasted_iota(jnp.int32, (1, LANES), 1) < HEAD_DIM

    def blockdiag(x):
        return jnp.concatenate([jnp.where(head0, x, 0.0), jnp.where(head0, 0.0, x)], axis=0).astype(BF16)

    pairs = range(n_pairs)
    sls = [slice(p * LANES, (p + 1) * LANES) for p in pairs]
    rows, g_tot, lr, bk, bkh, vb = [], [], [], [], [], []
    for k in range(n_sub):
        start = pl.multiple_of(jnp.where(fwd, k * ch, (n_sub - 1 - k) * ch), ch)
        rk = pl.ds(start, ch)
        rows.append(rk)
        lw = lw_ref[rk, :]
        lw_hi = lw.astype(BF16)
        lw_lo = (lw - lw_hi.astype(F32)).astype(BF16)
        cum = _dot(tri, lw_hi) + _dot(tri, lw_lo)
        tot = jnp.where(fwd, cum[ch - 1:ch], cum[0:1])
        e_neg = jnp.exp(-cum)
        gk = jnp.exp(tot)
        r_t = r_ref[rk, :].astype(F32) * jnp.exp(cum)
        a_t = -(kk_ref[rk, :].astype(F32) * jnp.exp(cum - lw))
        k_t = kd_ref[rk, :].astype(F32) * e_neg
        b_t = bd_ref[rk, :].astype(F32) * e_neg
        k_h = k_t * gk
        b_h = b_t * gk
        v_k = v_ref[rk, :].astype(F32)
        g_tot.append(gk)
        lr.append([jnp.concatenate([blockdiag(a_t[:, sl]), blockdiag(r_t[:, sl])], axis=0) for sl in sls])
        bk.append([jnp.concatenate([blockdiag(b_t[:, sl]), blockdiag(k_t[:, sl])], axis=0) for sl in sls])
        bkh.append([jnp.concatenate([blockdiag(b_h[:, sl]), blockdiag(k_h[:, sl])], axis=0) for sl in sls])
        vb.append([blockdiag(v_k[:, sl]) for sl in sls])

    chains = [(k, p) for k in range(n_sub) for p in pairs]
    n_levels = int(math.log2(ch))
    sc = {kp: _dot_nt(lr[kp[0]][kp[1]], bk[kp[0]][kp[1]]) for kp in chains}
    lab = {kp: jnp.where(strict, sc[kp][:pr, :pr], 0.0) for kp in chains}
    lak = {kp: jnp.where(strict, sc[kp][:pr, pr:], 0.0).astype(BF16) for kp in chains}
    mr = {kp: jnp.concatenate([jnp.where(incl, sc[kp][pr:, :pr], 0.0).astype(BF16),
                               jnp.where(incl, sc[kp][pr:, pr:], 0.0).astype(BF16)], axis=1) for kp in chains}
    t_inv = {kp: eye + lab[kp] for kp in chains}
    lab_b = {kp: lab[kp].astype(BF16) for kp in chains}
    pk = {kp: _dot(lab_b[kp], lab_b[kp]) for kp in chains}
    for lvl in range(1, n_levels):
        pb = {kp: pk[kp].astype(BF16) for kp in chains}
        if lvl < n_levels - 1:
            tp = {kp: _dot(jnp.concatenate([t_inv[kp].astype(BF16), pb[kp]], axis=0), pb[kp]) for kp in chains}
            t_inv = {kp: t_inv[kp] + tp[kp][:pr] for kp in chains}
            pk = {kp: tp[kp][pr:] for kp in chains}
        else:
            t_inv = {kp: t_inv[kp] + _dot(t_inv[kp].astype(BF16), pb[kp]) for kp in chains}
    t_b = {kp: t_inv[kp].astype(BF16) for kp in chains}
    lv = {kp: _dot(lak[kp], vb[kp[0]][kp[1]]) for kp in chains}

    st = [st_ref[p] for p in pairs]
    for k in range(n_sub):
        lrs = [_dot_nt(lr[k][p], st[p].astype(BF16)) for p in pairs]
        u = [_dot(t_b[(k, p)], (lrs[p][:pr] + lv[(k, p)]).astype(BF16)) for p in pairs]
        uv = [jnp.concatenate([u[p].astype(BF16), vb[k][p]], axis=0) for p in pairs]
        y = [lrs[p][pr:] + _dot(mr[(k, p)], uv[p]) for p in pairs]
        for p in pairs:
            y_ref[rows[k], sls[p]] = y[p][:ch] + y[p][ch:]
        st = [st[p] * g_tot[k][:, sls[p]] + _dot_tn(uv[p], bkh[k][p]) for p in pairs]
    for p in pairs:
        st_ref[p] = st[p]


def _wkv(r, v, kkn, lw, kd, bd):
    b, s, d = r.shape
    ch = min(WKV_CHUNK, s)
    rows = min(WKV_CHUNKS_PER_STEP * ch, s)
    nb = s // rows
    n_pairs = d // LANES

    def bidx(zdir, c):
        return jnp.where(zdir == 0, c, nb - 1 - c)

    shared = pl.BlockSpec((None, rows, d), lambda bi, zdir, c: (bi, bidx(zdir, c), 0))
    perdir = pl.BlockSpec((None, None, rows, d), lambda bi, zdir, c: (zdir, bi, bidx(zdir, c), 0))
    return pl.pallas_call(
        functools.partial(_wkv_kernel, ch=ch),
        grid=(b, 2, nb),
        in_specs=[shared, shared, shared, perdir, perdir, perdir],
        out_specs=perdir,
        out_shape=jax.ShapeDtypeStruct((2, b, s, d), F32),
        scratch_shapes=[pltpu.VMEM((n_pairs, LANES, LANES), F32)],
        compiler_params=_params(("parallel", "parallel", "arbitrary")),
        name="wkv7_chunked_scan",
    )(r, v, kkn, lw, kd, bd)


def _rwkv_post_kernel(y_ref, bonus_ref, gate_ref, h_ref, lnw_ref, lnb_ref, hs_ref, hem_ref, wo_ref, o_ref):
    ts = h_ref.shape[0]
    rb = ts // POST_SUB_BLOCKS if ts % POST_SUB_BLOCKS == 0 else ts
    subs = [pl.ds(i * rb, rb) for i in range(ts // rb)]
    y = [y_ref[0, r, :] + y_ref[1, r, :] for r in subs]
    mean = [_head_reduce(v, hs_ref, hem_ref) for v in y]
    dlt = [v - m for v, m in zip(y, mean)]
    var = [_head_reduce(v * v, hs_ref, hem_ref) for v in dlt]
    lnw = lnw_ref[...]
    lnb = lnb_ref[...]
    out = [((dv * lax.rsqrt(vr + GN_EPS) * lnw + lnb + bonus_ref[r, :]) * gate_ref[r, :]).astype(BF16)
           for dv, vr, r in zip(dlt, var, subs)]
    for v, r in zip(out, subs):
        o_ref[r, :] = h_ref[r, :] + _dot(v, wo_ref[...])


def _rwkv_post(y, bonus, gate, h, ln_w, ln_b, hs, hem, w_o):
    b, s, d = h.shape
    ts = min(512, s)
    tile = pl.BlockSpec((None, ts, d), lambda bi, i: (bi, i, 0))
    vec = pl.BlockSpec((1, d), lambda bi, i: (0, 0))
    mat = pl.BlockSpec((d, d), lambda bi, i: (0, 0))
    return pl.pallas_call(
        _rwkv_post_kernel,
        grid=(b, s // ts),
        in_specs=[pl.BlockSpec((2, None, ts, d), lambda bi, i: (0, bi, i, 0)),
                  tile, tile, tile, vec, vec,
                  pl.BlockSpec(hs.shape, lambda bi, i: (0, 0)), pl.BlockSpec(hem.shape, lambda bi, i: (0, 0)), mat],
        out_specs=tile,
        out_shape=jax.ShapeDtypeStruct((b, s, d), F32),
        compiler_params=_params(("parallel", "arbitrary")),
        name="rwkv_groupnorm_proj",
    )(y, bonus, gate, h, ln_w.reshape(1, d), ln_b.reshape(1, d), hs, hem, w_o.astype(BF16))


def _rwkv_layer(h, g, mu, w_rkv, w0, w1, w2, a0, a1, a2, g1, g2, k_k, k_a, r_k, ln_w, ln_b, w_o):
    d = h.shape[-1]
    head_of = np.arange(d) // HEAD_DIM
    ind = (head_of[:, None] == np.arange(LANES)[None, :]).astype(np.float32)
    hs = jnp.asarray(ind).astype(BF16)
    he = jnp.asarray(ind.T).astype(BF16)
    hem = jnp.asarray(ind.T / HEAD_DIM).astype(BF16)
    r, v, kkn, lw, kd, bd, bonus, gate = _rwkv_pre(
        h, g, mu, w_rkv, w0, w1, w2, a0, a1, a2, g1, g2, k_k, k_a, r_k, hs, he)
    y = _wkv(r, v, kkn, lw, kd, bd)
    return _rwkv_post(y, bonus, gate, h, ln_w, ln_b, hs, hem, w_o)


def kernel(x, norm_mix_g, norm_ffn_g, norm_final_g, fnet_w_o, fnet_b_o, rwkv_mu, rwkv_w_rkv, rwkv_w0, rwkv_w1, rwkv_w2, rwkv_a0, rwkv_a1, rwkv_a2, rwkv_g1, rwkv_g2, rwkv_k_k, rwkv_k_a, rwkv_r_k, rwkv_ln_w, rwkv_ln_b, rwkv_w_o, ffn_w_in, ffn_conv_w, ffn_conv_b, ffn_w_out):
    depth = norm_mix_g.shape[0]
    h = x
    for i in range(depth):
        j = i // 2
        if i % 2 == 0:
            h = _fnet_layer(h, norm_mix_g[i], fnet_w_o[j], fnet_b_o[j])
        else:
            h = _rwkv_layer(h, norm_mix_g[i], rwkv_mu[j], rwkv_w_rkv[j], rwkv_w0[j], rwkv_w1[j],
                            rwkv_w2[j], rwkv_a0[j], rwkv_a1[j], rwkv_a2[j], rwkv_g1[j], rwkv_g2[j],
                            rwkv_k_k[j], rwkv_k_a[j], rwkv_r_k[j], rwkv_ln_w[j], rwkv_ln_b[j],
                            rwkv_w_o[j])
        g_final = norm_final_g if i == depth - 1 else None
        h = _convffn_layer(h, norm_ffn_g[i], ffn_w_in[i], ffn_conv_w[i], ffn_conv_b[i],
                           ffn_w_out[i], g_final, row_block=FFN_ROW_BLOCK if i == 0 else FFN_ROW_BLOCK // 2)
    return h
```

```python
import functools
import math

import jax
import jax.numpy as jnp
import numpy as np
from jax import lax
from jax.experimental import pallas as pl
from jax.experimental.pallas import tpu as pltpu

HEAD_DIM = 64
FNET_GROUPS = 4
RMS_EPS = 1e-6
GN_EPS = 64e-5
L2_EPS = 1e-12

LANES = 128
SUBLANES = 8
WKV_CHUNK = 64
WKV_CHUNKS_PER_STEP = 4
WKV_GROUP_LANES = 128
FFN_ROW_BLOCK = 512
POST_SUB_BLOCKS = 2
PRE_SUB_BLOCKS = 2
PRE_ROWS = 512
VMEM_LIMIT_BYTES = 56 * 1024 * 1024

F32 = jnp.float32
BF16 = jnp.bfloat16


def _params(semantics):
    return pltpu.CompilerParams(dimension_semantics=semantics, vmem_limit_bytes=VMEM_LIMIT_BYTES)


def _rms(x, g):
    ms = jnp.mean(x * x, axis=-1, keepdims=True)
    return x * lax.rsqrt(ms + RMS_EPS) * g


def _dot(a, b):
    return jnp.dot(a, b, preferred_element_type=F32)


def _dot_nt(a, b):
    return lax.dot_general(a, b, (((1,), (1,)), ((), ())), preferred_element_type=F32)


def _dot_tn(a, b):
    return lax.dot_general(a, b, (((0,), (0,)), ((), ())), preferred_element_type=F32)


def _head_reduce(x, hs_ref, he_ref):
    part = _dot(x.astype(BF16), hs_ref[...])
    return _dot(part.astype(BF16), he_ref[...])


def _fnet_a_kernel(x_ref, g_ref, cs_ref, zc_ref, zs_ref):
    hn = _rms(x_ref[...], g_ref[...]).astype(BF16)
    gd = cs_ref.shape[0]
    for grp in range(hn.shape[1] // gd):
        sl = slice(grp * gd, (grp + 1) * gd)
        z = _dot(hn[:, sl], cs_ref[...])
        zc_ref[:, sl] = z[:, :gd].astype(BF16)
        zs_ref[:, sl] = z[:, gd:].astype(BF16)


def _fnet_b_kernel(c_ref, s_ref, zc_ref, zs_ref, wo_ref, bo_ref, x_ref, o_ref, *, scale):
    f = _dot(c_ref[...], zc_ref[...]) - _dot(s_ref[...], zs_ref[...])
    f = (f * scale).astype(BF16)
    o_ref[...] = x_ref[...] + _dot(f, wo_ref[...]) + bo_ref[...]


def _dft_mats(n):
    idx = np.arange(n)
    ang = (2.0 * np.pi / n) * ((idx[:, None] * idx[None, :]) % n)
    return jnp.asarray(np.cos(ang), F32).astype(BF16), jnp.asarray(np.sin(ang), F32).astype(BF16)


def _fnet_layer(x, g, w_o, b_o):
    b, s, d = x.shape
    gd = d // FNET_GROUPS
    t = b * s
    cc, sc = _dft_mats(gd)
    cs = jnp.concatenate([cc, sc], axis=1)
    cs_seq, ss_seq = _dft_mats(s)
    tm = min(1024, t)
    zc, zs = pl.pallas_call(
        _fnet_a_kernel,
        grid=(t // tm,),
        in_specs=[pl.BlockSpec((tm, d), lambda i: (i, 0)),
                  pl.BlockSpec((1, d), lambda i: (0, 0)),
                  pl.BlockSpec((gd, 2 * gd), lambda i: (0, 0))],
        out_specs=[pl.BlockSpec((tm, d), lambda i: (i, 0)),
                   pl.BlockSpec((tm, d), lambda i: (i, 0))],
        out_shape=[jax.ShapeDtypeStruct((t, d), BF16), jax.ShapeDtypeStruct((t, d), BF16)],
        compiler_params=_params(("parallel",)),
        name="fnet_channel_dft",
    )(x.reshape(t, d), g.reshape(1, d), cs)
    zc = zc.reshape(b, s, d)
    zs = zs.reshape(b, s, d)
    ts = min(512, s)
    scale = 1.0 / math.sqrt(s * gd)
    return pl.pallas_call(
        functools.partial(_fnet_b_kernel, scale=scale),
        grid=(b, s // ts),
        in_specs=[pl.BlockSpec((ts, s), lambda bi, i: (i, 0)),
                  pl.BlockSpec((ts, s), lambda bi, i: (i, 0)),
                  pl.BlockSpec((None, s, d), lambda bi, i: (bi, 0, 0)),
                  pl.BlockSpec((None, s, d), lambda bi, i: (bi, 0, 0)),
                  pl.BlockSpec((d, d), lambda bi, i: (0, 0)),
                  pl.BlockSpec((1, d), lambda bi, i: (0, 0)),
                  pl.BlockSpec((None, ts, d), lambda bi, i: (bi, i, 0))],
        out_specs=pl.BlockSpec((None, ts, d), lambda bi, i: (bi, i, 0)),
        out_shape=jax.ShapeDtypeStruct((b, s, d), F32),
        compiler_params=_params(("parallel", "arbitrary")),
        name="fnet_seq_dft_proj",
    )(cs_seq, ss_seq, zc, zs, w_o.astype(BF16), b_o.reshape(1, d), x)


def _ffn_act_kernel(h_ref, g_ref, wg_ref, wv_ref, cwg_ref, cwv_ref, cbg_ref, cbv_ref, act_ref,
                    hn_ref, ug_ref, uv_ref, *, row_block):
    @pl.when(pl.program_id(1) == 0)
    def _():
        hn_ref[...] = _rms(h_ref[...], g_ref[...]).astype(BF16)
        ug_ref[...] = jnp.zeros_like(ug_ref)
        uv_ref[...] = jnp.zeros_like(uv_ref)

    s, ck = ug_ref.shape
    rb = min(row_block, s)
    nb = s // rb
    zero_row = jnp.zeros((1, ck), F32)

    def edges(u_ref):
        prev = [zero_row if b == 0 else u_ref[pl.ds(b * rb - SUBLANES, SUBLANES), :][SUBLANES - 1:] for b in range(nb)]
        nxt = [zero_row if b == nb - 1 else u_ref[pl.ds((b + 1) * rb, SUBLANES), :][:1] for b in range(nb)]
        return prev, nxt

    edge_g = edges(ug_ref)
    edge_v = edges(uv_ref)
    row = lax.broadcasted_iota(jnp.int32, (rb, 1), 0)
    first = row == 0
    last = row == rb - 1
    for b in range(nb):
        rows = pl.ds(b * rb, rb)

        def conv(u_ref, edge, cw_ref, cb_ref):
            u = u_ref[rows, :]
            up = jnp.where(first, edge[0][b], pltpu.roll(u, 1, 0))
            un = jnp.where(last, edge[1][b], pltpu.roll(u, rb - 1, 0))
            cw = cw_ref[...]
            return up * cw[0:1] + u * cw[1:2] + un * cw[2:3] + cb_ref[...]

        ug = conv(ug_ref, edge_g, cwg_ref, cbg_ref)
        uv = conv(uv_ref, edge_v, cwv_ref, cbv_ref)
        act_ref[rows, :] = (ug * jax.nn.sigmoid(ug) * uv).astype(BF16)
        hb = hn_ref[rows, :]
        ug_ref[rows, :] = _dot(hb, wg_ref[...])
        uv_ref[rows, :] = _dot(hb, wv_ref[...])


def _ffn_out_kernel(act_ref, wo_ref, h_ref, gf_ref, o_ref, *, final_norm):
    act = jnp.concatenate([act_ref[j] for j in range(act_ref.shape[0])], axis=1)
    o = h_ref[...] + _dot(act, wo_ref[...])
    if final_norm:
        o = _rms(o, gf_ref[...])
    o_ref[...] = o


def _convffn_layer(h, g, w_in, conv_w, conv_b, w_out, g_final=None, row_block=FFN_ROW_BLOCK):
    b, s, d = h.shape
    dff = w_out.shape[0]
    ck = 256 if dff % 256 == 0 else dff
    nj = dff // ck
    w_in = w_in.astype(BF16).reshape(d, 2 * nj, ck).transpose(1, 0, 2)
    conv_b = conv_b.reshape(1, 2 * dff)
    mm = lambda j: jnp.minimum(j, nj - 1)
    ew = lambda j: jnp.maximum(j - 1, 0)
    act = pl.pallas_call(
        functools.partial(_ffn_act_kernel, row_block=row_block),
        grid=(b, nj + 1),
        in_specs=[pl.BlockSpec((None, s, d), lambda bi, j: (bi, 0, 0)),
                  pl.BlockSpec((1, d), lambda bi, j: (0, 0)),
                  pl.BlockSpec((None, d, ck), lambda bi, j: (mm(j), 0, 0)),
                  pl.BlockSpec((None, d, ck), lambda bi, j: (nj + mm(j), 0, 0)),
                  pl.BlockSpec((3, ck), lambda bi, j: (0, ew(j))),
                  pl.BlockSpec((3, ck), lambda bi, j: (0, nj + ew(j))),
                  pl.BlockSpec((1, ck), lambda bi, j: (0, ew(j))),
                  pl.BlockSpec((1, ck), lambda bi, j: (0, nj + ew(j)))],
        out_specs=pl.BlockSpec((None, None, s, ck), lambda bi, j: (bi, ew(j), 0, 0)),
        out_shape=jax.ShapeDtypeStruct((b, nj, s, ck), BF16),
        scratch_shapes=[pltpu.VMEM((s, d), BF16), pltpu.VMEM((s, ck), F32), pltpu.VMEM((s, ck), F32)],
        compiler_params=_params(("parallel", "arbitrary")),
        name="convffn_act",
    )(h, g.reshape(1, d), w_in, w_in, conv_w, conv_w, conv_b, conv_b)
    final_norm = g_final is not None
    gf = (g_final if final_norm else g).reshape(1, d)
    tm = min(1024, s)
    return pl.pallas_call(
        functools.partial(_ffn_out_kernel, final_norm=final_norm),
        grid=(b, s // tm),
        in_specs=[pl.BlockSpec((None, nj, tm, ck), lambda bi, i: (bi, 0, i, 0)),
                  pl.BlockSpec((dff, d), lambda bi, i: (0, 0)),
                  pl.BlockSpec((None, tm, d), lambda bi, i: (bi, i, 0)),
                  pl.BlockSpec((1, d), lambda bi, i: (0, 0))],
        out_specs=pl.BlockSpec((None, tm, d), lambda bi, i: (bi, i, 0)),
        out_shape=jax.ShapeDtypeStruct((b, s, d), F32),
        compiler_params=_params(("parallel", "arbitrary")),
        name="convffn_out_final" if final_norm else "convffn_out",
    )(act, w_out.astype(BF16), h, gf)


def _rwkv_pre_kernel(h_ref, hp_ref, hx_ref, g_ref, mu_ref, wrkv_ref, w1_ref, w2_ref, w0_ref,
                     a1_ref, a2_ref, a0_ref, g1_ref, g2_ref, kk_ref, ka_ref, rk_ref, hs_ref, he_ref,
                     r_out, v_out, kkn_out, lw_out, kd_out, bd_out, bonus_out, gate_out):
    i = pl.program_id(1)
    g = g_ref[...]
    hn = _rms(h_ref[...], g)
    ts = hn.shape[0]
    prev_edge = _rms(hp_ref[...], g)[SUBLANES - 1:SUBLANES]
    next_edge = _rms(hx_ref[...], g)[0:1]
    prev_edge = jnp.where(i == 0, 0.0, prev_edge)
    next_edge = jnp.where(i == pl.num_programs(1) - 1, 0.0, next_edge)
    row = lax.broadcasted_iota(jnp.int32, (ts, 1), 0)
    prev = jnp.where(row == 0, prev_edge, pltpu.roll(hn, 1, 0))
    nxt = jnp.where(row == ts - 1, next_edge, pltpu.roll(hn, ts - 1, 0))
    dp = prev - hn
    dn = nxt - hn
    mu = mu_ref[...]
    n_mix = mu.shape[0] // 2

    rb = ts // PRE_SUB_BLOCKS if ts % PRE_SUB_BLOCKS == 0 else ts
    subs = [slice(j * rb, (j + 1) * rb) for j in range(ts // rb)]

    def mix(m):
        return [(hn[sb] + dp[sb] * mu[m:m + 1] + dn[sb] * mu[n_mix + m:n_mix + m + 1]).astype(BF16) for sb in subs]

    r = [_dot(x, wrkv_ref[0]) for x in mix(0)]
    k = [_dot(x, wrkv_ref[1]) for x in mix(1)]
    v = [_dot(x, wrkv_ref[2]) for x in mix(2)]
    tw = [jnp.tanh(_dot(x, w1_ref[...])).astype(BF16) for x in mix(3)]
    ta = [_dot(x, a1_ref[...]).astype(BF16) for x in mix(4)]
    tg = [jax.nn.sigmoid(_dot(x, g1_ref[...])).astype(BF16) for x in mix(5)]
    gate = [_dot(x, g2_ref[...]) for x in tg]

    kraw = [x * kk_ref[...] for x in k]
    ssq = [_head_reduce(x * x, hs_ref, he_ref) for x in kraw]
    kkn = [x * lax.rsqrt(jnp.maximum(q, L2_EPS * L2_EPS)) for x, q in zip(kraw, ssq)]
    kka = [x * ka_ref[...] for x in k]
    kd_base = [x - y for x, y in zip(k, kka)]
    kd_sum = [jnp.zeros_like(x) for x in k]
    for zdir in range(2):
        xw = [w0_ref[zdir:zdir + 1] + _dot(x, w2_ref[zdir]) for x in tw]
        la = [_dot(x, a2_ref[zdir]) for x in ta]
        for j, sb in enumerate(subs):
            lw_out[zdir, sb, :] = jax.nn.sigmoid(xw[j]) * (-math.exp(-0.5))
            ag = jax.nn.sigmoid(a0_ref[zdir:zdir + 1] + la[j])
            kd = kd_base[j] + kka[j] * ag
            kd_sum[j] = kd_sum[j] + kd
            kd_out[zdir, sb, :] = kd.astype(BF16)
            bd_out[zdir, sb, :] = (kkn[j] * ag).astype(BF16)
    bsum = [_head_reduce(r[j] * kd_sum[j] * rk_ref[...], hs_ref, he_ref) for j in range(len(subs))]
    for j, sb in enumerate(subs):
        r_out[sb, :] = r[j].astype(BF16)
        v_out[sb, :] = v[j].astype(BF16)
        kkn_out[sb, :] = kkn[j].astype(BF16)
        bonus_out[sb, :] = (bsum[j] * v[j]).astype(bonus_out.dtype)
        gate_out[sb, :] = gate[j].astype(gate_out.dtype)


def _rwkv_pre(h, g, mu, w_rkv, w0, w1, w2, a0, a1, a2, g1, g2, k_k, k_a, r_k, hs, he):
    b, s, d = h.shape
    ts = min(PRE_ROWS, s)
    nt = s // ts
    nh8 = s // SUBLANES
    rdec = w1.shape[-1]
    raaa = a1.shape[-1]
    w1c = jnp.concatenate([w1[0], w1[1]], axis=1).astype(BF16)
    a1c = jnp.concatenate([a1[0], a1[1]], axis=1).astype(BF16)
    zw = jnp.zeros_like(w2[0])
    za = jnp.zeros_like(a2[0])
    w2p = jnp.stack([jnp.concatenate([w2[0], zw], 0), jnp.concatenate([zw, w2[1]], 0)]).astype(BF16)
    a2p = jnp.stack([jnp.concatenate([a2[0], za], 0), jnp.concatenate([za, a2[1]], 0)]).astype(BF16)
    full2 = lambda shape: pl.BlockSpec(shape, lambda bi, i: (0,) * len(shape), pipeline_mode=pl.Buffered(1))
    tile = pl.BlockSpec((None, ts, d), lambda bi, i: (bi, i, 0))
    tile2 = pl.BlockSpec((2, None, ts, d), lambda bi, i: (0, bi, i, 0))
    tpb = ts // SUBLANES
    in_specs = [
        tile,
        pl.BlockSpec((None, SUBLANES, d), lambda bi, i: (bi, jnp.maximum(i * tpb - 1, 0), 0)),
        pl.BlockSpec((None, SUBLANES, d), lambda bi, i: (bi, jnp.minimum((i + 1) * tpb, nh8 - 1), 0)),
        full2((1, d)), full2((2 * mu.shape[1], d)), full2((3, d, d)),
        full2((d, 2 * rdec)), full2((2, 2 * rdec, d)), full2((2, d)),
        full2((d, 2 * raaa)), full2((2, 2 * raaa, d)), full2((2, d)),
        full2(g1.shape), full2(g2.shape), full2((1, d)), full2((1, d)), full2((1, d)),
        full2(hs.shape), full2(he.shape),
    ]
    out_specs = [tile, tile, tile, tile2, tile2, tile2, tile, tile]
    bsd = (b, s, d)
    out_shape = [jax.ShapeDtypeStruct(bsd, BF16), jax.ShapeDtypeStruct(bsd, BF16),
                 jax.ShapeDtypeStruct(bsd, BF16), jax.ShapeDtypeStruct((2,) + bsd, F32),
                 jax.ShapeDtypeStruct((2,) + bsd, BF16), jax.ShapeDtypeStruct((2,) + bsd, BF16),
                 jax.ShapeDtypeStruct(bsd, BF16), jax.ShapeDtypeStruct(bsd, BF16)]
    return pl.pallas_call(
        _rwkv_pre_kernel,
        grid=(b, nt),
        in_specs=in_specs,
        out_specs=out_specs,
        out_shape=out_shape,
        compiler_params=_params(("parallel", "arbitrary")),
        name="rwkv_projections",
    )(h, h, h, g.reshape(1, d), mu.reshape(2 * mu.shape[1], d), w_rkv.astype(BF16),
      w1c, w2p, w0, a1c, a2p, a0, g1.astype(BF16), g2.astype(BF16),
      k_k.reshape(1, d), k_a.reshape(1, d), r_k.reshape(1, d), hs, he)


def _wkv_kernel(r_ref, v_ref, kk_ref, lw_ref, kd_ref, bd_ref, y_ref, st_ref, *, ch):
    zdir = pl.program_id(1)
    c = pl.program_id(2)
    gw = WKV_GROUP_LANES
    gh = gw // HEAD_DIM
    n_sub = r_ref.shape[0] // ch
    n_grp = r_ref.shape[1] // gw
    fwd = zdir == 0

    @pl.when(c == 0)
    def _():
        st_ref[...] = jnp.zeros_like(st_ref)

    sign = jnp.where(fwd, 1, -1)
    tt = lax.broadcasted_iota(jnp.int32, (ch, gh * ch), 0)
    jl = lax.broadcasted_iota(jnp.int32, (ch, gh * ch), 1)
    jj = jl - (jl // ch) * ch
    order = (tt - jj) * sign
    strict = order > 0
    incl = order >= 0
    eye = (tt == jj).astype(F32)
    r8 = lax.broadcasted_iota(jnp.int32, (ch, ch), 0)
    c8 = lax.broadcasted_iota(jnp.int32, (ch, ch), 1)
    tri = jnp.where((r8 - c8) * sign >= 0, 1.0, 0.0).astype(BF16)
    same_head = (lax.broadcasted_iota(jnp.int32, (gh * ch, gw), 0) // ch
                 == lax.broadcasted_iota(jnp.int32, (gh * ch, gw), 1) // HEAD_DIM)
    same_head_sq = (lax.broadcasted_iota(jnp.int32, (gw, gw), 0) // HEAD_DIM
                    == lax.broadcasted_iota(jnp.int32, (gw, gw), 1) // HEAD_DIM)

    def bd(x):
        xb = x.astype(BF16)
        return jnp.where(same_head, jnp.concatenate([xb] * gh, axis=0), jnp.zeros((), BF16))

    grps = range(n_grp)
    sls = [slice(g * gw, (g + 1) * gw) for g in grps]
    rows, g_tot, lr, bk, bkh, vw = [], [], [], [], [], []
    for k in range(n_sub):
        start = pl.multiple_of(jnp.where(fwd, k * ch, (n_sub - 1 - k) * ch), ch)
        rk = pl.ds(start, ch)
        rows.append(rk)
        lw = lw_ref[rk, :]
        lw_hi = lw.astype(BF16)
        lw_lo = (lw - lw_hi.astype(F32)).astype(BF16)
        cum = _dot(tri, lw_hi) + _dot(tri, lw_lo)
        tot = jnp.where(fwd, cum[ch - 1:ch], cum[0:1])
        e_neg = jnp.exp(-cum)
        gk = jnp.exp(tot)
        r_t = r_ref[rk, :].astype(F32) * jnp.exp(cum)
        a_t = -(kk_ref[rk, :].astype(F32) * jnp.exp(cum - lw))
        k_t = kd_ref[rk, :].astype(F32) * e_neg
        b_t = bd_ref[rk, :].astype(F32) * e_neg
        k_h = k_t * gk
        b_h = b_t * gk
        v_k = v_ref[rk, :]
        g_tot.append(gk)
        lr.append([jnp.concatenate([a_t[:, sl], r_t[:, sl]], axis=0).astype(BF16) for sl in sls])
        bk.append([jnp.concatenate([bd(b_t[:, sl]), bd(k_t[:, sl])], axis=0) for sl in sls])
        bkh.append([jnp.concatenate([b_h[:, sl], k_h[:, sl]], axis=0).astype(BF16) for sl in sls])
        vw.append([v_k[:, sl] for sl in sls])

    chains = [(k, g) for k in range(n_sub) for g in grps]
    n_levels = int(math.log2(ch))
    wd = gh * ch
    sc = {kg: _dot_nt(lr[kg[0]][kg[1]], bk[kg[0]][kg[1]]) for kg in chains}
    lab = {kg: jnp.where(strict, sc[kg][:ch, :wd], 0.0) for kg in chains}
    lak = {kg: jnp.where(strict, sc[kg][:ch, wd:], 0.0).astype(BF16) for kg in chains}
    mr = {kg: jnp.concatenate([jnp.where(incl, sc[kg][ch:, :wd], 0.0).astype(BF16),
                               jnp.where(incl, sc[kg][ch:, wd:], 0.0).astype(BF16)], axis=1) for kg in chains}
    vbd = {kg: bd(vw[kg[0]][kg[1]]) for kg in chains}
    t_inv = {kg: eye + lab[kg] for kg in chains}
    pk = {kg: _dot(lab[kg].astype(BF16), bd(lab[kg])) for kg in chains}
    for lvl in range(1, n_levels):
        pb = {kg: bd(pk[kg]) for kg in chains}
        if lvl < n_levels - 1:
            tp = {kg: _dot(jnp.concatenate([t_inv[kg], pk[kg]], axis=0).astype(BF16), pb[kg]) for kg in chains}
            t_inv = {kg: t_inv[kg] + tp[kg][:ch] for kg in chains}
            pk = {kg: tp[kg][ch:] for kg in chains}
        else:
            t_inv = {kg: t_inv[kg] + _dot(t_inv[kg].astype(BF16), pb[kg]) for kg in chains}
    t_b = {kg: t_inv[kg].astype(BF16) for kg in chains}
    lv = {kg: _dot(lak[kg], vbd[kg]) for kg in chains}

    st = [st_ref[g] for g in grps]
    for k in range(n_sub):
        lrs = [_dot_nt(lr[k][g], st[g].astype(BF16)) for g in grps]
        u = [_dot(t_b[(k, g)], bd(lrs[g][:ch] + lv[(k, g)])) for g in grps]
        uvb = [jnp.concatenate([bd(u[g]), vbd[(k, g)]], axis=0) for g in grps]
        for g in grps:
            y_ref[rows[k], sls[g]] = (lrs[g][ch:] + _dot(mr[(k, g)], uvb[g])).astype(y_ref.dtype)
        uv = [jnp.concatenate([u[g].astype(BF16), vw[k][g]], axis=0) for g in grps]
        st = [st[g] * g_tot[k][:, sls[g]] + jnp.where(same_head_sq, _dot_tn(uv[g], bkh[k][g]), 0.0) for g in grps]
    for g in grps:
        st_ref[g] = st[g]


def _wkv(r, v, kkn, lw, kd, bd):
    b, s, d = r.shape
    ch = min(WKV_CHUNK, s)
    rows = min(WKV_CHUNKS_PER_STEP * ch, s)
    nb = s // rows
    assert ch == HEAD_DIM, "wide-tile masks assume chunk length == head_dim"
    n_grp = d // WKV_GROUP_LANES

    def bidx(zdir, c):
        return jnp.where(zdir == 0, c, nb - 1 - c)

    shared = pl.BlockSpec((None, rows, d), lambda bi, zdir, c: (bi, bidx(zdir, c), 0))
    perdir = pl.BlockSpec((None, None, rows, d), lambda bi, zdir, c: (zdir, bi, bidx(zdir, c), 0))
    return pl.pallas_call(
        functools.partial(_wkv_kernel, ch=ch),
        grid=(b, 2, nb),
        in_specs=[shared, shared, shared, perdir, perdir, perdir],
        out_specs=perdir,
        out_shape=jax.ShapeDtypeStruct((2, b, s, d), BF16),
        scratch_shapes=[pltpu.VMEM((n_grp, WKV_GROUP_LANES, WKV_GROUP_LANES), F32)],
        compiler_params=_params(("parallel", "parallel", "arbitrary")),
        name="wkv7_chunked_scan",
    )(r, v, kkn, lw, kd, bd)


def _rwkv_post_kernel(y_ref, bonus_ref, gate_ref, h_ref, lnw_ref, lnb_ref, hs_ref, hem_ref, wo_ref, o_ref):
    ts = h_ref.shape[0]
    rb = ts // POST_SUB_BLOCKS if ts % POST_SUB_BLOCKS == 0 else ts
    subs = [pl.ds(i * rb, rb) for i in range(ts // rb)]
    y = [y_ref[0, r, :] + y_ref[1, r, :] for r in subs]
    mean = [_head_reduce(v, hs_ref, hem_ref) for v in y]
    dlt = [v - m for v, m in zip(y, mean)]
    var = [_head_reduce(v * v, hs_ref, hem_ref) for v in dlt]
    lnw = lnw_ref[...]
    lnb = lnb_ref[...]
    out = [((dv * lax.rsqrt(vr + GN_EPS) * lnw + lnb + bonus_ref[r, :]) * gate_ref[r, :]).astype(BF16)
           for dv, vr, r in zip(dlt, var, subs)]
    for v, r in zip(out, subs):
        o_ref[r, :] = h_ref[r, :] + _dot(v, wo_ref[...])


def _rwkv_post(y, bonus, gate, h, ln_w, ln_b, hs, hem, w_o):
    b, s, d = h.shape
    ts = min(512, s)
    tile = pl.BlockSpec((None, ts, d), lambda bi, i: (bi, i, 0))
    vec = pl.BlockSpec((1, d), lambda bi, i: (0, 0))
    mat = pl.BlockSpec((d, d), lambda bi, i: (0, 0))
    return pl.pallas_call(
        _rwkv_post_kernel,
        grid=(b, s // ts),
        in_specs=[pl.BlockSpec((2, None, ts, d), lambda bi, i: (0, bi, i, 0)),
                  tile, tile, tile, vec, vec,
                  pl.BlockSpec(hs.shape, lambda bi, i: (0, 0)), pl.BlockSpec(hem.shape, lambda bi, i: (0, 0)), mat],
        out_specs=tile,
        out_shape=jax.ShapeDtypeStruct((b, s, d), F32),
        compiler_params=_params(("parallel", "arbitrary")),
        name="rwkv_groupnorm_proj",
    )(y, bonus, gate, h, ln_w.reshape(1, d), ln_b.reshape(1, d), hs, hem, w_o.astype(BF16))


def _rwkv_layer(h, g, mu, w_rkv, w0, w1, w2, a0, a1, a2, g1, g2, k_k, k_a, r_k, ln_w, ln_b, w_o):
    d = h.shape[-1]
    head_of = np.arange(d) // HEAD_DIM
    ind = (head_of[:, None] == np.arange(LANES)[None, :]).astype(np.float32)
    hs = jnp.asarray(ind).astype(BF16)
    he = jnp.asarray(ind.T).astype(BF16)
    hem = jnp.asarray(ind.T / HEAD_DIM).astype(BF16)
    r, v, kkn, lw, kd, bd, bonus, gate = _rwkv_pre(
        h, g, mu, w_rkv, w0, w1, w2, a0, a1, a2, g1, g2, k_k, k_a, r_k, hs, he)
    y = _wkv(r, v, kkn, lw, kd, bd)
    return _rwkv_post(y, bonus, gate, h, ln_w, ln_b, hs, hem, w_o)


def kernel(x, norm_mix_g, norm_ffn_g, norm_final_g, fnet_w_o, fnet_b_o, rwkv_mu, rwkv_w_rkv, rwkv_w0, rwkv_w1, rwkv_w2, rwkv_a0, rwkv_a1, rwkv_a2, rwkv_g1, rwkv_g2, rwkv_k_k, rwkv_k_a, rwkv_r_k, rwkv_ln_w, rwkv_ln_b, rwkv_w_o, ffn_w_in, ffn_conv_w, ffn_conv_b, ffn_w_out):
    depth = norm_mix_g.shape[0]
    h = x
    for i in range(depth):
        j = i // 2
        if i % 2 == 0:
            h = _fnet_layer(h, norm_mix_g[i], fnet_w_o[j], fnet_b_o[j])
        else:
            h = _rwkv_layer(h, norm_mix_g[i], rwkv_mu[j], rwkv_w_rkv[j], rwkv_w0[j], rwkv_w1[j],
                            rwkv_w2[j], rwkv_a0[j], rwkv_a1[j], rwkv_a2[j], rwkv_g1[j], rwkv_g2[j],
                            rwkv_k_k[j], rwkv_k_a[j], rwkv_r_k[j], rwkv_ln_w[j], rwkv_ln_b[j],
                            rwkv_w_o[j])
        g_final = norm_final_g if i == depth - 1 else None
        h = _convffn_layer(h, norm_ffn_g[i], ffn_w_in[i], ffn_conv_w[i], ffn_conv_b[i],
                           ffn_w_out[i], g_final, row_block=FFN_ROW_BLOCK if i == 0 else FFN_ROW_BLOCK // 2)
    return h
```

```python
import functools
import math

import jax
import jax.numpy as jnp
import numpy as np
from jax import lax
from jax.experimental import pallas as pl
from jax.experimental.pallas import tpu as pltpu

HEAD_DIM = 64
FNET_GROUPS = 4
RMS_EPS = 1e-6
GN_EPS = 64e-5
L2_EPS = 1e-12

LANES = 128
SUBLANES = 8
WKV_CHUNK = 64
WKV_CHUNKS_PER_STEP = 4
WKV_GROUP_LANES = 128
FNET_SEQ_ROWS = 512
FFN_ROW_BLOCK = 512
POST_SUB_BLOCKS = 2
PRE_SUB_BLOCKS = 2
PRE_ROWS = 512
VMEM_LIMIT_BYTES = 56 * 1024 * 1024

F32 = jnp.float32
BF16 = jnp.bfloat16


def _params(semantics):
    return pltpu.CompilerParams(dimension_semantics=semantics, vmem_limit_bytes=VMEM_LIMIT_BYTES)


def _rms(x, g):
    ms = jnp.mean(x * x, axis=-1, keepdims=True)
    return x * lax.rsqrt(ms + RMS_EPS) * g


def _dot(a, b):
    return jnp.dot(a, b, preferred_element_type=F32)


def _dot_nt(a, b):
    return lax.dot_general(a, b, (((1,), (1,)), ((), ())), preferred_element_type=F32)


def _dot_tn(a, b):
    return lax.dot_general(a, b, (((0,), (0,)), ((), ())), preferred_element_type=F32)


def _head_reduce(x, hs_ref, he_ref):
    part = _dot(x.astype(BF16), hs_ref[...])
    return _dot(part.astype(BF16), he_ref[...])


def _fnet_a_kernel(x_ref, g_ref, cs_ref, zc_ref, zs_ref):
    hn = _rms(x_ref[...], g_ref[...]).astype(BF16)
    gd = cs_ref.shape[0]
    for grp in range(hn.shape[1] // gd):
        sl = slice(grp * gd, (grp + 1) * gd)
        z = _dot(hn[:, sl], cs_ref[...])
        zc_ref[:, sl] = z[:, :gd].astype(BF16)
        zs_ref[:, sl] = z[:, gd:].astype(BF16)


def _fnet_b_kernel(c_ref, s_ref, jrev_ref, zc_ref, zs_ref, wo_ref, bo_ref, xlo_ref, xhi_ref, o_ref, *, scale):
    ts = xlo_ref.shape[0]
    a = _dot(c_ref[...], zc_ref[...])
    b = _dot(s_ref[...], zs_ref[...])
    direct = ((a[:ts] - b[:ts]) * scale).astype(BF16)
    g = ((a + b) * scale).astype(BF16)
    mirror = _dot(jrev_ref[...], g[:ts])
    row = lax.broadcasted_iota(jnp.int32, (ts, 1), 0)
    mirror = jnp.where(row == 0, g[ts:ts + 1].astype(F32), mirror).astype(BF16)
    wo = wo_ref[...]
    o_ref[0] = xlo_ref[...] + _dot(direct, wo) + bo_ref[...]
    o_ref[1] = xhi_ref[...] + _dot(mirror, wo) + bo_ref[...]


def _dft_mats(n):
    idx = np.arange(n)
    ang = (2.0 * np.pi / n) * ((idx[:, None] * idx[None, :]) % n)
    return np.cos(ang), np.sin(ang)


def _bf16_const(x):
    return jnp.asarray(x, F32).astype(BF16)


def _fnet_layer(x, g, w_o, b_o):
    b, s, d = x.shape
    gd = d // FNET_GROUPS
    t = b * s
    cc, sc = _dft_mats(gd)
    cs = _bf16_const(np.concatenate([cc, sc], axis=1))
    tm = min(1024, t)
    zc, zs = pl.pallas_call(
        _fnet_a_kernel,
        grid=(t // tm,),
        in_specs=[pl.BlockSpec((tm, d), lambda i: (i, 0)),
                  pl.BlockSpec((1, d), lambda i: (0, 0)),
                  pl.BlockSpec((gd, 2 * gd), lambda i: (0, 0))],
        out_specs=[pl.BlockSpec((tm, d), lambda i: (i, 0)),
                   pl.BlockSpec((tm, d), lambda i: (i, 0))],
        out_shape=[jax.ShapeDtypeStruct((t, d), BF16), jax.ShapeDtypeStruct((t, d), BF16)],
        compiler_params=_params(("parallel",)),
        name="fnet_channel_dft",
    )(x.reshape(t, d), g.reshape(1, d), cs)
    zc = zc.reshape(b, s, d)
    zs = zs.reshape(b, s, d)
    half = s // 2
    ts = min(FNET_SEQ_ROWS, half)
    nb = half // ts
    halo = 2 * SUBLANES
    cseq, sseq = _dft_mats(s)
    cext = _bf16_const(np.stack([cseq[i * ts:i * ts + ts + halo] for i in range(nb)]))
    sext = _bf16_const(np.stack([sseq[i * ts:i * ts + ts + halo] for i in range(nb)]))
    jrev = np.zeros((ts, ts), np.float32)
    jrev[np.arange(1, ts), ts - np.arange(1, ts)] = 1.0
    scale = 1.0 / math.sqrt(s * gd)
    return pl.pallas_call(
        functools.partial(_fnet_b_kernel, scale=scale),
        grid=(b, nb),
        in_specs=[pl.BlockSpec((None, ts + halo, s), lambda bi, i: (i, 0, 0)),
                  pl.BlockSpec((None, ts + halo, s), lambda bi, i: (i, 0, 0)),
                  pl.BlockSpec((ts, ts), lambda bi, i: (0, 0)),
                  pl.BlockSpec((None, s, d), lambda bi, i: (bi, 0, 0)),
                  pl.BlockSpec((None, s, d), lambda bi, i: (bi, 0, 0)),
                  pl.BlockSpec((d, d), lambda bi, i: (0, 0)),
                  pl.BlockSpec((1, d), lambda bi, i: (0, 0)),
                  pl.BlockSpec((None, ts, d), lambda bi, i: (bi, i, 0)),
                  pl.BlockSpec((None, ts, d), lambda bi, i: (bi, 2 * nb - 1 - i, 0))],
        out_specs=pl.BlockSpec((None, 2, ts, d), lambda bi, i: (bi, 0, i, 0)),
        out_shape=jax.ShapeDtypeStruct((b, 2, half, d), F32),
        compiler_params=_params(("parallel", "arbitrary")),
        name="fnet_seq_dft_proj",
    )(cext, sext, _bf16_const(jrev), zc, zs, w_o.astype(BF16), b_o.reshape(1, d), x, x)


def _ffn_act_kernel(h_ref, g_ref, wg_ref, wv_ref, cwg_ref, cwv_ref, cbg_ref, cbv_ref, act_ref,
                    hn_ref, ug_ref, uv_ref, *, row_block, mirror_rows):
    @pl.when(pl.program_id(1) == 0)
    def _():
        if mirror_rows:
            half = h_ref.shape[1]
            nbh = half // mirror_rows
            hn_ref[pl.ds(0, half), :] = _rms(h_ref[0], g_ref[...]).astype(BF16)
            for q in range(nbh):
                src = pl.ds((nbh - 1 - q) * mirror_rows, mirror_rows)
                hn_ref[pl.ds(half + q * mirror_rows, mirror_rows), :] = _rms(h_ref[1, src, :], g_ref[...]).astype(BF16)
        else:
            hn_ref[...] = _rms(h_ref[...], g_ref[...]).astype(BF16)
        ug_ref[...] = jnp.zeros_like(ug_ref)
        uv_ref[...] = jnp.zeros_like(uv_ref)

    s, ck = ug_ref.shape
    rb = min(row_block, s)
    nb = s // rb
    zero_row = jnp.zeros((1, ck), F32)

    def edges(u_ref):
        prev = [zero_row if b == 0 else u_ref[pl.ds(b * rb - SUBLANES, SUBLANES), :][SUBLANES - 1:] for b in range(nb)]
        nxt = [zero_row if b == nb - 1 else u_ref[pl.ds((b + 1) * rb, SUBLANES), :][:1] for b in range(nb)]
        return prev, nxt

    edge_g = edges(ug_ref)
    edge_v = edges(uv_ref)
    row = lax.broadcasted_iota(jnp.int32, (rb, 1), 0)
    first = row == 0
    last = row == rb - 1
    wg = wg_ref[...].astype(BF16)
    wv = wv_ref[...].astype(BF16)
    for b in range(nb):
        rows = pl.ds(b * rb, rb)

        def conv(u_ref, edge, cw_ref, cb_ref):
            u = u_ref[rows, :]
            up = jnp.where(first, edge[0][b], pltpu.roll(u, 1, 0))
            un = jnp.where(last, edge[1][b], pltpu.roll(u, rb - 1, 0))
            cw = cw_ref[...]
            return up * cw[0:1] + u * cw[1:2] + un * cw[2:3] + cb_ref[...]

        ug = conv(ug_ref, edge_g, cwg_ref, cbg_ref)
        uv = conv(uv_ref, edge_v, cwv_ref, cbv_ref)
        act_ref[rows, :] = (ug * jax.nn.sigmoid(ug) * uv).astype(BF16)
        hb = hn_ref[rows, :]
        ug_ref[rows, :] = _dot(hb, wg)
        uv_ref[rows, :] = _dot(hb, wv)


def _ffn_out_kernel(act_ref, wo_ref, h_ref, gf_ref, o_ref, *, final_norm):
    o = h_ref[...] + _dot(act_ref[...], wo_ref[...])
    if final_norm:
        o = _rms(o, gf_ref[...])
    o_ref[...] = o


def _convffn_layer(h, g, w_in, conv_w, conv_b, w_out, g_final=None, row_block=FFN_ROW_BLOCK, mirrored=False):
    if mirrored:
        b, _, half, d = h.shape
        s = 2 * half
        mts = min(FNET_SEQ_ROWS, half)
    else:
        b, s, d = h.shape
        mts = 0
    dff = w_out.shape[0]
    ck = 256 if dff % 256 == 0 else dff
    nj = dff // ck
    conv_b = conv_b.reshape(1, 2 * dff)
    mm = lambda j: jnp.minimum(j, nj - 1)
    ew = lambda j: jnp.maximum(j - 1, 0)
    act = pl.pallas_call(
        functools.partial(_ffn_act_kernel, row_block=row_block, mirror_rows=mts),
        grid=(b, nj + 1),
        in_specs=[pl.BlockSpec((None, 2, s // 2, d), lambda bi, j: (bi, 0, 0, 0)) if mirrored
                  else pl.BlockSpec((None, s, d), lambda bi, j: (bi, 0, 0)),
                  pl.BlockSpec((1, d), lambda bi, j: (0, 0)),
                  pl.BlockSpec((d, ck), lambda bi, j: (0, mm(j))),
                  pl.BlockSpec((d, ck), lambda bi, j: (0, nj + mm(j))),
                  pl.BlockSpec((3, ck), lambda bi, j: (0, ew(j))),
                  pl.BlockSpec((3, ck), lambda bi, j: (0, nj + ew(j))),
                  pl.BlockSpec((1, ck), lambda bi, j: (0, ew(j))),
                  pl.BlockSpec((1, ck), lambda bi, j: (0, nj + ew(j)))],
        out_specs=pl.BlockSpec((None, s, ck), lambda bi, j: (bi, 0, ew(j))),
        out_shape=jax.ShapeDtypeStruct((b, s, dff), BF16),
        scratch_shapes=[pltpu.VMEM((s, d), BF16), pltpu.VMEM((s, ck), F32), pltpu.VMEM((s, ck), F32)],
        compiler_params=_params(("parallel", "arbitrary")),
        name="convffn_act",
    )(h, g.reshape(1, d), w_in, w_in, conv_w, conv_w, conv_b, conv_b)
    final_norm = g_final is not None
    gf = (g_final if final_norm else g).reshape(1, d)
    if mirrored:
        tm = mts
        nbh = (s // 2) // tm
        h_spec = pl.BlockSpec((None, None, tm, d),
                              lambda bi, i: (bi, i // nbh, jnp.where(i < nbh, i, 2 * nbh - 1 - i), 0))
    else:
        tm = min(1024, s)
        h_spec = pl.BlockSpec((None, tm, d), lambda bi, i: (bi, i, 0))
    return pl.pallas_call(
        functools.partial(_ffn_out_kernel, final_norm=final_norm),
        grid=(b, s // tm),
        in_specs=[pl.BlockSpec((None, tm, dff), lambda bi, i: (bi, i, 0)),
                  pl.BlockSpec((dff, d), lambda bi, i: (0, 0)),
                  h_spec,
                  pl.BlockSpec((1, d), lambda bi, i: (0, 0))],
        out_specs=pl.BlockSpec((None, tm, d), lambda bi, i: (bi, i, 0)),
        out_shape=jax.ShapeDtypeStruct((b, s, d), F32),
        compiler_params=_params(("parallel", "arbitrary")),
        name="convffn_out_final" if final_norm else "convffn_out",
    )(act, w_out.astype(BF16), h, gf)


def _rwkv_pre_kernel(h_ref, hp_ref, hx_ref, g_ref, mu_ref, wrkv_ref, w1_ref, w2_ref, w0_ref,
                     a1_ref, a2_ref, a0_ref, g1_ref, g2_ref, kk_ref, ka_ref, rk_ref, hs_ref, he_ref,
                     r_out, v_out, kkn_out, lw_out, kd_out, bd_out, bonus_out, gate_out):
    i = pl.program_id(1)
    g = g_ref[...]
    hn = _rms(h_ref[...], g)
    ts = hn.shape[0]
    prev_edge = _rms(hp_ref[...], g)[SUBLANES - 1:SUBLANES]
    next_edge = _rms(hx_ref[...], g)[0:1]
    prev_edge = jnp.where(i == 0, 0.0, prev_edge)
    next_edge = jnp.where(i == pl.num_programs(1) - 1, 0.0, next_edge)
    row = lax.broadcasted_iota(jnp.int32, (ts, 1), 0)
    prev = jnp.where(row == 0, prev_edge, pltpu.roll(hn, 1, 0))
    nxt = jnp.where(row == ts - 1, next_edge, pltpu.roll(hn, ts - 1, 0))
    dp = prev - hn
    dn = nxt - hn
    mu = mu_ref[...]
    n_mix = mu.shape[0] // 2

    rb = ts // PRE_SUB_BLOCKS if ts % PRE_SUB_BLOCKS == 0 else ts
    subs = [slice(j * rb, (j + 1) * rb) for j in range(ts // rb)]

    def mix(m):
        return [(hn[sb] + dp[sb] * mu[m:m + 1] + dn[sb] * mu[n_mix + m:n_mix + m + 1]).astype(BF16) for sb in subs]

    r = [_dot(x, wrkv_ref[0]) for x in mix(0)]
    k = [_dot(x, wrkv_ref[1]) for x in mix(1)]
    v = [_dot(x, wrkv_ref[2]) for x in mix(2)]
    tw = [jnp.tanh(_dot(x, w1_ref[...])).astype(BF16) for x in mix(3)]
    ta = [_dot(x, a1_ref[...]).astype(BF16) for x in mix(4)]
    tg = [jax.nn.sigmoid(_dot(x, g1_ref[...])).astype(BF16) for x in mix(5)]
    gate = [_dot(x, g2_ref[...]) for x in tg]

    kraw = [x * kk_ref[...] for x in k]
    ssq = [_head_reduce(x * x, hs_ref, he_ref) for x in kraw]
    kkn = [x * lax.rsqrt(jnp.maximum(q, L2_EPS * L2_EPS)) for x, q in zip(kraw, ssq)]
    kka = [x * ka_ref[...] for x in k]
    kd_base = [x - y for x, y in zip(k, kka)]
    kd_sum = [jnp.zeros_like(x) for x in k]
    for zdir in range(2):
        xw = [w0_ref[zdir:zdir + 1] + _dot(x, w2_ref[zdir]) for x in tw]
        la = [_dot(x, a2_ref[zdir]) for x in ta]
        for j, sb in enumerate(subs):
            lw_out[zdir, sb, :] = jax.nn.sigmoid(xw[j]) * (-math.exp(-0.5))
            ag = jax.nn.sigmoid(a0_ref[zdir:zdir + 1] + la[j])
            kd = kd_base[j] + kka[j] * ag
            kd_sum[j] = kd_sum[j] + kd
            kd_out[zdir, sb, :] = kd.astype(BF16)
            bd_out[zdir, sb, :] = (kkn[j] * ag).astype(BF16)
    bsum = [_head_reduce(r[j] * kd_sum[j] * rk_ref[...], hs_ref, he_ref) for j in range(len(subs))]
    for j, sb in enumerate(subs):
        r_out[sb, :] = r[j].astype(BF16)
        v_out[sb, :] = v[j].astype(BF16)
        kkn_out[sb, :] = kkn[j].astype(BF16)
        bonus_out[sb, :] = (bsum[j] * v[j]).astype(bonus_out.dtype)
        gate_out[sb, :] = gate[j].astype(gate_out.dtype)


def _rwkv_pre(h, g, mu, w_rkv, w0, w1, w2, a0, a1, a2, g1, g2, k_k, k_a, r_k, hs, he):
    b, s, d = h.shape
    ts = min(PRE_ROWS, s)
    nt = s // ts
    nh8 = s // SUBLANES
    rdec = w1.shape[-1]
    raaa = a1.shape[-1]
    w1c = jnp.concatenate([w1[0], w1[1]], axis=1).astype(BF16)
    a1c = jnp.concatenate([a1[0], a1[1]], axis=1).astype(BF16)
    zw = jnp.zeros_like(w2[0])
    za = jnp.zeros_like(a2[0])
    w2p = jnp.stack([jnp.concatenate([w2[0], zw], 0), jnp.concatenate([zw, w2[1]], 0)]).astype(BF16)
    a2p = jnp.stack([jnp.concatenate([a2[0], za], 0), jnp.concatenate([za, a2[1]], 0)]).astype(BF16)
    full2 = lambda shape: pl.BlockSpec(shape, lambda bi, i: (0,) * len(shape), pipeline_mode=pl.Buffered(1))
    tile = pl.BlockSpec((None, ts, d), lambda bi, i: (bi, i, 0))
    tile2 = pl.BlockSpec((2, None, ts, d), lambda bi, i: (0, bi, i, 0))
    tpb = ts // SUBLANES
    in_specs = [
        tile,
        pl.BlockSpec((None, SUBLANES, d), lambda bi, i: (bi, jnp.maximum(i * tpb - 1, 0), 0)),
        pl.BlockSpec((None, SUBLANES, d), lambda bi, i: (bi, jnp.minimum((i + 1) * tpb, nh8 - 1), 0)),
        full2((1, d)), full2((2 * mu.shape[1], d)), full2((3, d, d)),
        full2((d, 2 * rdec)), full2((2, 2 * rdec, d)), full2((2, d)),
        full2((d, 2 * raaa)), full2((2, 2 * raaa, d)), full2((2, d)),
        full2(g1.shape), full2(g2.shape), full2((1, d)), full2((1, d)), full2((1, d)),
        full2(hs.shape), full2(he.shape),
    ]
    out_specs = [tile, tile, tile, tile2, tile2, tile2, tile, tile]
    bsd = (b, s, d)
    out_shape = [jax.ShapeDtypeStruct(bsd, BF16), jax.ShapeDtypeStruct(bsd, BF16),
                 jax.ShapeDtypeStruct(bsd, BF16), jax.ShapeDtypeStruct((2,) + bsd, F32),
                 jax.ShapeDtypeStruct((2,) + bsd, BF16), jax.ShapeDtypeStruct((2,) + bsd, BF16),
                 jax.ShapeDtypeStruct(bsd, BF16), jax.ShapeDtypeStruct(bsd, BF16)]
    return pl.pallas_call(
        _rwkv_pre_kernel,
        grid=(b, nt),
        in_specs=in_specs,
        out_specs=out_specs,
        out_shape=out_shape,
        compiler_params=_params(("parallel", "arbitrary")),
        name="rwkv_projections",
    )(h, h, h, g.reshape(1, d), mu.reshape(2 * mu.shape[1], d), w_rkv.astype(BF16),
      w1c, w2p, w0, a1c, a2p, a0, g1.astype(BF16), g2.astype(BF16),
      k_k.reshape(1, d), k_a.reshape(1, d), r_k.reshape(1, d), hs, he)


def _wkv_kernel(r_ref, v_ref, kk_ref, lw_ref, kd_ref, bd_ref, y_ref, st_ref, *, ch):
    zdir = pl.program_id(1)
    c = pl.program_id(2)
    gw = WKV_GROUP_LANES
    gh = gw // HEAD_DIM
    n_sub = r_ref.shape[0] // ch
    n_grp = r_ref.shape[1] // gw
    fwd = zdir == 0

    @pl.when(c == 0)
    def _():
        st_ref[...] = jnp.zeros_like(st_ref)

    sign = jnp.where(fwd, 1, -1)
    tt = lax.broadcasted_iota(jnp.int32, (ch, gh * ch), 0)
    jl = lax.broadcasted_iota(jnp.int32, (ch, gh * ch), 1)
    jj = jl - (jl // ch) * ch
    order = (tt - jj) * sign
    strict = order > 0
    incl = order >= 0
    eye = (tt == jj).astype(F32)
    r8 = lax.broadcasted_iota(jnp.int32, (ch, ch), 0)
    c8 = lax.broadcasted_iota(jnp.int32, (ch, ch), 1)
    tri = jnp.where((r8 - c8) * sign >= 0, 1.0, 0.0).astype(BF16)
    same_head = (lax.broadcasted_iota(jnp.int32, (gh * ch, gw), 0) // ch
                 == lax.broadcasted_iota(jnp.int32, (gh * ch, gw), 1) // HEAD_DIM)
    same_head_sq = (lax.broadcasted_iota(jnp.int32, (gw, gw), 0) // HEAD_DIM
                    == lax.broadcasted_iota(jnp.int32, (gw, gw), 1) // HEAD_DIM)

    def bd(x):
        xb = x.astype(BF16)
        return jnp.where(same_head, jnp.concatenate([xb] * gh, axis=0), jnp.zeros((), BF16))

    grps = range(n_grp)
    sls = [slice(g * gw, (g + 1) * gw) for g in grps]
    rows, g_tot, lr, bk, bkh, vw = [], [], [], [], [], []
    for k in range(n_sub):
        start = pl.multiple_of(jnp.where(fwd, k * ch, (n_sub - 1 - k) * ch), ch)
        rk = pl.ds(start, ch)
        rows.append(rk)
        lw = lw_ref[rk, :]
        lw_hi = lw.astype(BF16)
        lw_lo = (lw - lw_hi.astype(F32)).astype(BF16)
        cum = _dot(tri, lw_hi) + _dot(tri, lw_lo)
        tot = jnp.where(fwd, cum[ch - 1:ch], cum[0:1])
        e_neg = jnp.exp(-cum)
        gk = jnp.exp(tot)
        r_t = r_ref[rk, :].astype(F32) * jnp.exp(cum)
        a_t = -(kk_ref[rk, :].astype(F32) * jnp.exp(cum - lw))
        k_t = kd_ref[rk, :].astype(F32) * e_neg
        b_t = bd_ref[rk, :].astype(F32) * e_neg
        k_h = k_t * gk
        b_h = b_t * gk
        v_k = v_ref[rk, :]
        g_tot.append(gk)
        lr.append([jnp.concatenate([a_t[:, sl], r_t[:, sl]], axis=0).astype(BF16) for sl in sls])
        bk.append([jnp.concatenate([bd(b_t[:, sl]), bd(k_t[:, sl])], axis=0) for sl in sls])
        bkh.append([jnp.concatenate([b_h[:, sl], k_h[:, sl]], axis=0).astype(BF16) for sl in sls])
        vw.append([v_k[:, sl] for sl in sls])

    chains = [(k, g) for k in range(n_sub) for g in grps]
    n_levels = int(math.log2(ch))
    wd = gh * ch
    sc = {kg: _dot_nt(lr[kg[0]][kg[1]], bk[kg[0]][kg[1]]) for kg in chains}
    lab = {kg: jnp.where(strict, sc[kg][:ch, :wd], 0.0) for kg in chains}
    lak = {kg: jnp.where(strict, sc[kg][:ch, wd:], 0.0).astype(BF16) for kg in chains}
    mr = {kg: jnp.concatenate([jnp.where(incl, sc[kg][ch:, :wd], 0.0).astype(BF16),
                               jnp.where(incl, sc[kg][ch:, wd:], 0.0).astype(BF16)], axis=1) for kg in chains}
    vbd = {kg: bd(vw[kg[0]][kg[1]]) for kg in chains}
    t_inv = {kg: eye + lab[kg] for kg in chains}
    pk = {kg: _dot(lab[kg].astype(BF16), bd(lab[kg])) for kg in chains}
    for lvl in range(1, n_levels):
        pb = {kg: bd(pk[kg]) for kg in chains}
        if lvl < n_levels - 1:
            tp = {kg: _dot(jnp.concatenate([t_inv[kg], pk[kg]], axis=0).astype(BF16), pb[kg]) for kg in chains}
            t_inv = {kg: t_inv[kg] + tp[kg][:ch] for kg in chains}
            pk = {kg: tp[kg][ch:] for kg in chains}
        else:
            t_inv = {kg: t_inv[kg] + _dot(t_inv[kg].astype(BF16), pb[kg]) for kg in chains}
    t_b = {kg: t_inv[kg].astype(BF16) for kg in chains}
    lv = {kg: _dot(lak[kg], vbd[kg]) for kg in chains}

    st = [st_ref[g] for g in grps]
    for k in range(n_sub):
        lrs = [_dot_nt(lr[k][g], st[g].astype(BF16)) for g in grps]
        u = [_dot(t_b[(k, g)], bd(lrs[g][:ch] + lv[(k, g)])) for g in grps]
        uvb = [jnp.concatenate([bd(u[g]), vbd[(k, g)]], axis=0) for g in grps]
        for g in grps:
            y_ref[rows[k], sls[g]] = (lrs[g][ch:] + _dot(mr[(k, g)], uvb[g])).astype(y_ref.dtype)
        uv = [jnp.concatenate([u[g].astype(BF16), vw[k][g]], axis=0) for g in grps]
        st = [st[g] * g_tot[k][:, sls[g]] + jnp.where(same_head_sq, _dot_tn(uv[g], bkh[k][g]), 0.0) for g in grps]
    for g in grps:
        st_ref[g] = st[g]


def _wkv(r, v, kkn, lw, kd, bd):
    b, s, d = r.shape
    ch = min(WKV_CHUNK, s)
    rows = min(WKV_CHUNKS_PER_STEP * ch, s)
    nb = s // rows
    assert ch == HEAD_DIM, "wide-tile masks assume chunk length == head_dim"
    n_grp = d // WKV_GROUP_LANES

    def bidx(zdir, c):
        return jnp.where(zdir == 0, c, nb - 1 - c)

    shared = pl.BlockSpec((None, rows, d), lambda bi, zdir, c: (bi, bidx(zdir, c), 0))
    perdir = pl.BlockSpec((None, None, rows, d), lambda bi, zdir, c: (zdir, bi, bidx(zdir, c), 0))
    return pl.pallas_call(
        functools.partial(_wkv_kernel, ch=ch),
        grid=(b, 2, nb),
        in_specs=[shared, shared, shared, perdir, perdir, perdir],
        out_specs=perdir,
        out_shape=jax.ShapeDtypeStruct((2, b, s, d), BF16),
        scratch_shapes=[pltpu.VMEM((n_grp, WKV_GROUP_LANES, WKV_GROUP_LANES), F32)],
        compiler_params=_params(("parallel", "parallel", "arbitrary")),
        name="wkv7_chunked_scan",
    )(r, v, kkn, lw, kd, bd)


def _rwkv_post_kernel(y_ref, bonus_ref, gate_ref, h_ref, lnw_ref, lnb_ref, hs_ref, hem_ref, wo_ref, o_ref):
    ts = h_ref.shape[0]
    rb = ts // POST_SUB_BLOCKS if ts % POST_SUB_BLOCKS == 0 else ts
    subs = [pl.ds(i * rb, rb) for i in range(ts // rb)]
    y = [y_ref[0, r, :] + y_ref[1, r, :] for r in subs]
    mean = [_head_reduce(v, hs_ref, hem_ref) for v in y]
    dlt = [v - m for v, m in zip(y, mean)]
    var = [_head_reduce(v * v, hs_ref, hem_ref) for v in dlt]
    lnw = lnw_ref[...]
    lnb = lnb_ref[...]
    out = [((dv * lax.rsqrt(vr + GN_EPS) * lnw + lnb + bonus_ref[r, :]) * gate_ref[r, :]).astype(BF16)
           for dv, vr, r in zip(dlt, var, subs)]
    for v, r in zip(out, subs):
        o_ref[r, :] = h_ref[r, :] + _dot(v, wo_ref[...])


def _rwkv_post(y, bonus, gate, h, ln_w, ln_b, hs, hem, w_o):
    b, s, d = h.shape
    ts = min(512, s)
    tile = pl.BlockSpec((None, ts, d), lambda bi, i: (bi, i, 0))
    vec = pl.BlockSpec((1, d), lambda bi, i: (0, 0))
    mat = pl.BlockSpec((d, d), lambda bi, i: (0, 0))
    return pl.pallas_call(
        _rwkv_post_kernel,
        grid=(b, s // ts),
        in_specs=[pl.BlockSpec((2, None, ts, d), lambda bi, i: (0, bi, i, 0)),
                  tile, tile, tile, vec, vec,
                  pl.BlockSpec(hs.shape, lambda bi, i: (0, 0)), pl.BlockSpec(hem.shape, lambda bi, i: (0, 0)), mat],
        out_specs=tile,
        out_shape=jax.ShapeDtypeStruct((b, s, d), F32),
        compiler_params=_params(("parallel", "arbitrary")),
        name="rwkv_groupnorm_proj",
    )(y, bonus, gate, h, ln_w.reshape(1, d), ln_b.reshape(1, d), hs, hem, w_o.astype(BF16))


def _rwkv_layer(h, g, mu, w_rkv, w0, w1, w2, a0, a1, a2, g1, g2, k_k, k_a, r_k, ln_w, ln_b, w_o):
    d = h.shape[-1]
    head_of = np.arange(d) // HEAD_DIM
    ind = (head_of[:, None] == np.arange(LANES)[None, :]).astype(np.float32)
    hs = jnp.asarray(ind).astype(BF16)
    he = jnp.asarray(ind.T).astype(BF16)
    hem = jnp.asarray(ind.T / HEAD_DIM).astype(BF16)
    r, v, kkn, lw, kd, bd, bonus, gate = _rwkv_pre(
        h, g, mu, w_rkv, w0, w1, w2, a0, a1, a2, g1, g2, k_k, k_a, r_k, hs, he)
    y = _wkv(r, v, kkn, lw, kd, bd)
    return _rwkv_post(y, bonus, gate, h, ln_w, ln_b, hs, hem, w_o)


def kernel(x, norm_mix_g, norm_ffn_g, norm_final_g, fnet_w_o, fnet_b_o, rwkv_mu, rwkv_w_rkv, rwkv_w0, rwkv_w1, rwkv_w2, rwkv_a0, rwkv_a1, rwkv_a2, rwkv_g1, rwkv_g2, rwkv_k_k, rwkv_k_a, rwkv_r_k, rwkv_ln_w, rwkv_ln_b, rwkv_w_o, ffn_w_in, ffn_conv_w, ffn_conv_b, ffn_w_out):
    depth = norm_mix_g.shape[0]
    h = x
    for i in range(depth):
        j = i // 2
        if i % 2 == 0:
            h = _fnet_layer(h, norm_mix_g[i], fnet_w_o[j], fnet_b_o[j])
        else:
            h = _rwkv_layer(h, norm_mix_g[i], rwkv_mu[j], rwkv_w_rkv[j], rwkv_w0[j], rwkv_w1[j],
                            rwkv_w2[j], rwkv_a0[j], rwkv_a1[j], rwkv_a2[j], rwkv_g1[j], rwkv_g2[j],
                            rwkv_k_k[j], rwkv_k_a[j], rwkv_r_k[j], rwkv_ln_w[j], rwkv_ln_b[j],
                            rwkv_w_o[j])
        g_final = norm_final_g if i == depth - 1 else None
        h = _convffn_layer(h, norm_ffn_g[i], ffn_w_in[i], ffn_conv_w[i], ffn_conv_b[i],
                           ffn_w_out[i], g_final, mirrored=(i % 2 == 0))
    return h
```

```python
import functools
import math

import jax
import jax.numpy as jnp
import numpy as np
from jax import lax
from jax.experimental import pallas as pl
from jax.experimental.pallas import tpu as pltpu

HEAD_DIM = 64
FNET_GROUPS = 4
RMS_EPS = 1e-6
GN_EPS = 64e-5
L2_EPS = 1e-12

LANES = 128
SUBLANES = 8
WKV_CHUNK = 64
WKV_CHUNKS_PER_STEP = 4
WKV_GROUP_LANES = 128
FNET_SEQ_ROWS = 512
FFN_ROW_BLOCK = 512
FFN_OUT_ROWS = 512
POST_SUB_BLOCKS = 2
PRE_SUB_BLOCKS = 2
PRE_ROWS = 512
VMEM_LIMIT_BYTES = 56 * 1024 * 1024

F32 = jnp.float32
BF16 = jnp.bfloat16


def _params(semantics):
    return pltpu.CompilerParams(dimension_semantics=semantics, vmem_limit_bytes=VMEM_LIMIT_BYTES)


def _rms(x, g):
    ms = jnp.mean(x * x, axis=-1, keepdims=True)
    return x * lax.rsqrt(ms + RMS_EPS) * g


def _dot(a, b):
    return jnp.dot(a, b, preferred_element_type=F32)


def _dot_nt(a, b):
    return lax.dot_general(a, b, (((1,), (1,)), ((), ())), preferred_element_type=F32)


def _dot_tn(a, b):
    return lax.dot_general(a, b, (((0,), (0,)), ((), ())), preferred_element_type=F32)


def _head_reduce(x, hs_ref, he_ref):
    part = _dot(x.astype(BF16), hs_ref[...])
    return _dot(part.astype(BF16), he_ref[...])


def _fnet_a_kernel(x_ref, g_ref, cs_ref, zc_ref, zs_ref):
    hn = _rms(x_ref[...], g_ref[...]).astype(BF16)
    gd = cs_ref.shape[0]
    for grp in range(hn.shape[1] // gd):
        sl = slice(grp * gd, (grp + 1) * gd)
        z = _dot(hn[:, sl], cs_ref[...])
        zc_ref[:, sl] = z[:, :gd].astype(BF16)
        zs_ref[:, sl] = z[:, gd:].astype(BF16)


def _fnet_b_kernel(c_ref, s_ref, jrev_ref, zc_ref, zs_ref, wo_ref, bo_ref, xlo_ref, xhi_ref, o_ref, *, scale):
    ts = xlo_ref.shape[0]
    a = _dot(c_ref[...], zc_ref[...])
    b = _dot(s_ref[...], zs_ref[...])
    direct = ((a[:ts] - b[:ts]) * scale).astype(BF16)
    g = ((a + b) * scale).astype(BF16)
    mirror = _dot(jrev_ref[...], g[:ts])
    row = lax.broadcasted_iota(jnp.int32, (ts, 1), 0)
    mirror = jnp.where(row == 0, g[ts:ts + 1].astype(F32), mirror).astype(BF16)
    wo = wo_ref[...]
    o_ref[0] = xlo_ref[...] + _dot(direct, wo) + bo_ref[...]
    o_ref[1] = xhi_ref[...] + _dot(mirror, wo) + bo_ref[...]


def _dft_mats(n):
    idx = np.arange(n)
    ang = (2.0 * np.pi / n) * ((idx[:, None] * idx[None, :]) % n)
    return np.cos(ang), np.sin(ang)


def _bf16_const(x):
    return jnp.asarray(x, F32).astype(BF16)


def _fnet_layer(x, g, w_o, b_o):
    b, s, d = x.shape
    gd = d // FNET_GROUPS
    t = b * s
    cc, sc = _dft_mats(gd)
    cs = _bf16_const(np.concatenate([cc, sc], axis=1))
    tm = min(1024, t)
    zc, zs = pl.pallas_call(
        _fnet_a_kernel,
        grid=(t // tm,),
        in_specs=[pl.BlockSpec((tm, d), lambda i: (i, 0)),
                  pl.BlockSpec((1, d), lambda i: (0, 0)),
                  pl.BlockSpec((gd, 2 * gd), lambda i: (0, 0))],
        out_specs=[pl.BlockSpec((tm, d), lambda i: (i, 0)),
                   pl.BlockSpec((tm, d), lambda i: (i, 0))],
        out_shape=[jax.ShapeDtypeStruct((t, d), BF16), jax.ShapeDtypeStruct((t, d), BF16)],
        compiler_params=_params(("parallel",)),
        name="fnet_channel_dft",
    )(x.reshape(t, d), g.reshape(1, d), cs)
    zc = zc.reshape(b, s, d)
    zs = zs.reshape(b, s, d)
    half = s // 2
    ts = min(FNET_SEQ_ROWS, half)
    nb = half // ts
    halo = 2 * SUBLANES
    cseq, sseq = _dft_mats(s)
    cext = _bf16_const(np.stack([cseq[i * ts:i * ts + ts + halo] for i in range(nb)]))
    sext = _bf16_const(np.stack([sseq[i * ts:i * ts + ts + halo] for i in range(nb)]))
    jrev = np.zeros((ts, ts), np.float32)
    jrev[np.arange(1, ts), ts - np.arange(1, ts)] = 1.0
    scale = 1.0 / math.sqrt(s * gd)
    return pl.pallas_call(
        functools.partial(_fnet_b_kernel, scale=scale),
        grid=(b, nb),
        in_specs=[pl.BlockSpec((None, ts + halo, s), lambda bi, i: (i, 0, 0)),
                  pl.BlockSpec((None, ts + halo, s), lambda bi, i: (i, 0, 0)),
                  pl.BlockSpec((ts, ts), lambda bi, i: (0, 0)),
                  pl.BlockSpec((None, s, d), lambda bi, i: (bi, 0, 0)),
                  pl.BlockSpec((None, s, d), lambda bi, i: (bi, 0, 0)),
                  pl.BlockSpec((d, d), lambda bi, i: (0, 0)),
                  pl.BlockSpec((1, d), lambda bi, i: (0, 0)),
                  pl.BlockSpec((None, ts, d), lambda bi, i: (bi, i, 0)),
                  pl.BlockSpec((None, ts, d), lambda bi, i: (bi, 2 * nb - 1 - i, 0))],
        out_specs=pl.BlockSpec((None, 2, ts, d), lambda bi, i: (bi, 0, i, 0)),
        out_shape=jax.ShapeDtypeStruct((b, 2, half, d), F32),
        compiler_params=_params(("parallel", "arbitrary")),
        name="fnet_seq_dft_proj",
    )(cext, sext, _bf16_const(jrev), zc, zs, w_o.astype(BF16), b_o.reshape(1, d), x, x)


def _ffn_in_kernel(h_ref, g_ref, wg_ref, wv_ref, cwg_ref, cbg_ref, sg_ref, uv_ref, hn_ref, ug_ref,
                   *, row_block, mirror_rows):
    j = pl.program_id(1)

    @pl.when(j == 0)
    def _():
        if mirror_rows:
            half = h_ref.shape[1]
            nbh = half // mirror_rows
            hn_ref[pl.ds(0, half), :] = _rms(h_ref[0], g_ref[...]).astype(BF16)
            for q in range(nbh):
                src = pl.ds((nbh - 1 - q) * mirror_rows, mirror_rows)
                hn_ref[pl.ds(half + q * mirror_rows, mirror_rows), :] = _rms(h_ref[1, src, :], g_ref[...]).astype(BF16)
        else:
            hn_ref[...] = _rms(h_ref[...], g_ref[...]).astype(BF16)
        ug_ref[...] = jnp.zeros_like(ug_ref)

    s, ck = ug_ref.shape
    rb = min(row_block, s)
    nb = s // rb
    zero_row = jnp.zeros((1, ck), F32)
    prev = [zero_row if b == 0 else ug_ref[pl.ds(b * rb - SUBLANES, SUBLANES), :][SUBLANES - 1:] for b in range(nb)]
    nxt = [zero_row if b == nb - 1 else ug_ref[pl.ds((b + 1) * rb, SUBLANES), :][:1] for b in range(nb)]
    row = lax.broadcasted_iota(jnp.int32, (rb, 1), 0)
    first = row == 0
    last = row == rb - 1
    wg = wg_ref[...].astype(BF16)
    wv = wv_ref[...].astype(BF16)
    cw = cwg_ref[...]
    for b in range(nb):
        rows = pl.ds(b * rb, rb)
        u = ug_ref[rows, :]
        up = jnp.where(first, prev[b], pltpu.roll(u, 1, 0))
        un = jnp.where(last, nxt[b], pltpu.roll(u, rb - 1, 0))
        gate = up * cw[0:1] + u * cw[1:2] + un * cw[2:3] + cbg_ref[...]
        sg_ref[rows, :] = (gate * jax.nn.sigmoid(gate)).astype(BF16)
        hb = hn_ref[rows, :]
        ug_ref[rows, :] = _dot(hb, wg)
        uv_ref[rows, :] = _dot(hb, wv).astype(BF16)


def _ffn_out_kernel(sg_ref, uv_ref, vp_ref, vn_ref, cw_ref, cb_ref, wo_ref, h_ref, gf_ref,
                    o_ref, act_ref, *, final_norm, ck):
    i = pl.program_id(1)
    tm, dff = uv_ref.shape
    hr = vp_ref.shape[0]
    at_start = i == 0
    at_end = i == pl.num_programs(1) - 1
    row = lax.broadcasted_iota(jnp.int32, (tm, 1), 0)
    first = row == 0
    last = row == tm - 1

    for c in range(dff // ck):
        cols = slice(c * ck, (c + 1) * ck)
        u = uv_ref[:, cols].astype(F32)
        prev_edge = jnp.where(at_start, 0.0, vp_ref[hr - 1:hr, cols].astype(F32))
        next_edge = jnp.where(at_end, 0.0, vn_ref[0:1, cols].astype(F32))
        up = jnp.where(first, prev_edge, pltpu.roll(u, 1, 0))
        un = jnp.where(last, next_edge, pltpu.roll(u, tm - 1, 0))
        cw = cw_ref[:, cols]
        val = up * cw[0:1] + u * cw[1:2] + un * cw[2:3] + cb_ref[:, cols]
        act_ref[:, cols] = (sg_ref[:, cols].astype(F32) * val).astype(BF16)
    o = h_ref[...] + _dot(act_ref[...], wo_ref[...])
    if final_norm:
        o = _rms(o, gf_ref[...])
    o_ref[...] = o


def _convffn_layer(h, g, w_in, conv_w, conv_b, w_out, g_final=None, mirrored=False):
    if mirrored:
        b, _, half, d = h.shape
        s = 2 * half
        mts = min(FNET_SEQ_ROWS, half)
    else:
        b, s, d = h.shape
        mts = 0
    dff = w_out.shape[0]
    ck = 256 if dff % 256 == 0 else dff
    nj = dff // ck
    conv_b = conv_b.reshape(1, 2 * dff)
    mm = lambda j: jnp.minimum(j, nj - 1)
    ew = lambda j: jnp.maximum(j - 1, 0)
    cg, uv = pl.pallas_call(
        functools.partial(_ffn_in_kernel, row_block=FFN_ROW_BLOCK, mirror_rows=mts),
        grid=(b, nj + 1),
        in_specs=[pl.BlockSpec((None, 2, s // 2, d), lambda bi, j: (bi, 0, 0, 0)) if mirrored
                  else pl.BlockSpec((None, s, d), lambda bi, j: (bi, 0, 0)),
                  pl.BlockSpec((1, d), lambda bi, j: (0, 0)),
                  pl.BlockSpec((d, ck), lambda bi, j: (0, mm(j))),
                  pl.BlockSpec((d, ck), lambda bi, j: (0, nj + mm(j))),
                  pl.BlockSpec((3, ck), lambda bi, j: (0, ew(j))),
                  pl.BlockSpec((1, ck), lambda bi, j: (0, ew(j)))],
        out_specs=[pl.BlockSpec((None, s, ck), lambda bi, j: (bi, 0, ew(j))),
                   pl.BlockSpec((None, s, ck), lambda bi, j: (bi, 0, mm(j)))],
        out_shape=[jax.ShapeDtypeStruct((b, s, dff), BF16), jax.ShapeDtypeStruct((b, s, dff), BF16)],
        scratch_shapes=[pltpu.VMEM((s, d), BF16), pltpu.VMEM((s, ck), F32)],
        compiler_params=_params(("parallel", "arbitrary")),
        name="convffn_in",
    )(h, g.reshape(1, d), w_in, w_in, conv_w, conv_b)
    final_norm = g_final is not None
    gf = (g_final if final_norm else g).reshape(1, d)
    tm = mts if mirrored else min(FFN_OUT_ROWS, s)
    nt = s // tm
    if mirrored:
        nbh = (s // 2) // tm
        h_spec = pl.BlockSpec((None, None, tm, d),
                              lambda bi, i: (bi, i // nbh, jnp.where(i < nbh, i, 2 * nbh - 1 - i), 0))
    else:
        h_spec = pl.BlockSpec((None, tm, d), lambda bi, i: (bi, i, 0))
    hr = 2 * SUBLANES
    tph = tm // hr
    tile = pl.BlockSpec((None, tm, dff), lambda bi, i: (bi, i, 0))
    halo_prev = pl.BlockSpec((None, hr, dff), lambda bi, i: (bi, jnp.maximum(i * tph - 1, 0), 0))
    halo_next = pl.BlockSpec((None, hr, dff), lambda bi, i: (bi, jnp.minimum((i + 1) * tph, s // hr - 1), 0))
    const = lambda shape: pl.BlockSpec(shape, lambda bi, i: (0,) * len(shape), pipeline_mode=pl.Buffered(1))
    return pl.pallas_call(
        functools.partial(_ffn_out_kernel, final_norm=final_norm, ck=ck),
        grid=(b, nt),
        in_specs=[tile, tile, halo_prev, halo_next,
                  pl.BlockSpec((3, dff), lambda bi, i: (0, 1), pipeline_mode=pl.Buffered(1)),
                  pl.BlockSpec((1, dff), lambda bi, i: (0, 1), pipeline_mode=pl.Buffered(1)),
                  const((dff, d)), h_spec, const((1, d))],
        out_specs=pl.BlockSpec((None, tm, d), lambda bi, i: (bi, i, 0)),
        out_shape=jax.ShapeDtypeStruct((b, s, d), F32),
        scratch_shapes=[pltpu.VMEM((tm, dff), BF16)],
        compiler_params=_params(("parallel", "arbitrary")),
        name="convffn_out_final" if final_norm else "convffn_out",
    )(cg, uv, uv, uv, conv_w, conv_b, w_out.astype(BF16), h, gf)


def _rwkv_pre_kernel(h_ref, hp_ref, hx_ref, g_ref, mu_ref, wrkv_ref, w1_ref, w2_ref, w0_ref,
                     a1_ref, a2_ref, a0_ref, g1_ref, g2_ref, kk_ref, ka_ref, rk_ref, hs_ref, he_ref,
                     r_out, v_out, kkn_out, lw_out, kd_out, bd_out, bonus_out, gate_out):
    i = pl.program_id(1)
    g = g_ref[...]
    hn = _rms(h_ref[...], g)
    ts = hn.shape[0]
    prev_edge = _rms(hp_ref[...], g)[SUBLANES - 1:SUBLANES]
    next_edge = _rms(hx_ref[...], g)[0:1]
    prev_edge = jnp.where(i == 0, 0.0, prev_edge)
    next_edge = jnp.where(i == pl.num_programs(1) - 1, 0.0, next_edge)
    row = lax.broadcasted_iota(jnp.int32, (ts, 1), 0)
    prev = jnp.where(row == 0, prev_edge, pltpu.roll(hn, 1, 0))
    nxt = jnp.where(row == ts - 1, next_edge, pltpu.roll(hn, ts - 1, 0))
    dp = prev - hn
    dn = nxt - hn
    mu = mu_ref[...]
    n_mix = mu.shape[0] // 2

    rb = ts // PRE_SUB_BLOCKS if ts % PRE_SUB_BLOCKS == 0 else ts
    subs = [slice(j * rb, (j + 1) * rb) for j in range(ts // rb)]

    def mix(m):
        return [(hn[sb] + dp[sb] * mu[m:m + 1] + dn[sb] * mu[n_mix + m:n_mix + m + 1]).astype(BF16) for sb in subs]

    r = [_dot(x, wrkv_ref[0]) for x in mix(0)]
    k = [_dot(x, wrkv_ref[1]) for x in mix(1)]
    v = [_dot(x, wrkv_ref[2]) for x in mix(2)]
    tw = [jnp.tanh(_dot(x, w1_ref[...])).astype(BF16) for x in mix(3)]
    ta = [_dot(x, a1_ref[...]).astype(BF16) for x in mix(4)]
    tg = [jax.nn.sigmoid(_dot(x, g1_ref[...])).astype(BF16) for x in mix(5)]
    gate = [_dot(x, g2_ref[...]) for x in tg]

    kraw = [x * kk_ref[...] for x in k]
    ssq = [_head_reduce(x * x, hs_ref, he_ref) for x in kraw]
    kkn = [x * lax.rsqrt(jnp.maximum(q, L2_EPS * L2_EPS)) for x, q in zip(kraw, ssq)]
    kka = [x * ka_ref[...] for x in k]
    kd_base = [x - y for x, y in zip(k, kka)]
    kd_sum = [jnp.zeros_like(x) for x in k]
    for zdir in range(2):
        xw = [w0_ref[zdir:zdir + 1] + _dot(x, w2_ref[zdir]) for x in tw]
        la = [_dot(x, a2_ref[zdir]) for x in ta]
        for j, sb in enumerate(subs):
            lw_out[zdir, sb, :] = jax.nn.sigmoid(xw[j]) * (-math.exp(-0.5))
            ag = jax.nn.sigmoid(a0_ref[zdir:zdir + 1] + la[j])
            kd = kd_base[j] + kka[j] * ag
            kd_sum[j] = kd_sum[j] + kd
            kd_out[zdir, sb, :] = kd.astype(BF16)
            bd_out[zdir, sb, :] = (kkn[j] * ag).astype(BF16)
    bsum = [_head_reduce(r[j] * kd_sum[j] * rk_ref[...], hs_ref, he_ref) for j in range(len(subs))]
    for j, sb in enumerate(subs):
        r_out[sb, :] = r[j].astype(BF16)
        v_out[sb, :] = v[j].astype(BF16)
        kkn_out[sb, :] = kkn[j].astype(BF16)
        bonus_out[sb, :] = (bsum[j] * v[j]).astype(bonus_out.dtype)
        gate_out[sb, :] = gate[j].astype(gate_out.dtype)


def _rwkv_pre(h, g, mu, w_rkv, w0, w1, w2, a0, a1, a2, g1, g2, k_k, k_a, r_k, hs, he):
    b, s, d = h.shape
    ts = min(PRE_ROWS, s)
    nt = s // ts
    nh8 = s // SUBLANES
    rdec = w1.shape[-1]
    raaa = a1.shape[-1]
    w1c = jnp.concatenate([w1[0], w1[1]], axis=1).astype(BF16)
    a1c = jnp.concatenate([a1[0], a1[1]], axis=1).astype(BF16)
    zw = jnp.zeros_like(w2[0])
    za = jnp.zeros_like(a2[0])
    w2p = jnp.stack([jnp.concatenate([w2[0], zw], 0), jnp.concatenate([zw, w2[1]], 0)]).astype(BF16)
    a2p = jnp.stack([jnp.concatenate([a2[0], za], 0), jnp.concatenate([za, a2[1]], 0)]).astype(BF16)
    full2 = lambda shape: pl.BlockSpec(shape, lambda bi, i: (0,) * len(shape), pipeline_mode=pl.Buffered(1))
    tile = pl.BlockSpec((None, ts, d), lambda bi, i: (bi, i, 0))
    tile2 = pl.BlockSpec((2, None, ts, d), lambda bi, i: (0, bi, i, 0))
    tpb = ts // SUBLANES
    in_specs = [
        tile,
        pl.BlockSpec((None, SUBLANES, d), lambda bi, i: (bi, jnp.maximum(i * tpb - 1, 0), 0)),
        pl.BlockSpec((None, SUBLANES, d), lambda bi, i: (bi, jnp.minimum((i + 1) * tpb, nh8 - 1), 0)),
        full2((1, d)), full2((2 * mu.shape[1], d)), full2((3, d, d)),
        full2((d, 2 * rdec)), full2((2, 2 * rdec, d)), full2((2, d)),
        full2((d, 2 * raaa)), full2((2, 2 * raaa, d)), full2((2, d)),
        full2(g1.shape), full2(g2.shape), full2((1, d)), full2((1, d)), full2((1, d)),
        full2(hs.shape), full2(he.shape),
    ]
    out_specs = [tile, tile, tile, tile2, tile2, tile2, tile, tile]
    bsd = (b, s, d)
    out_shape = [jax.ShapeDtypeStruct(bsd, BF16), jax.ShapeDtypeStruct(bsd, BF16),
                 jax.ShapeDtypeStruct(bsd, BF16), jax.ShapeDtypeStruct((2,) + bsd, F32),
                 jax.ShapeDtypeStruct((2,) + bsd, BF16), jax.ShapeDtypeStruct((2,) + bsd, BF16),
                 jax.ShapeDtypeStruct(bsd, BF16), jax.ShapeDtypeStruct(bsd, BF16)]
    return pl.pallas_call(
        _rwkv_pre_kernel,
        grid=(b, nt),
        in_specs=in_specs,
        out_specs=out_specs,
        out_shape=out_shape,
        compiler_params=_params(("parallel", "arbitrary")),
        name="rwkv_projections",
    )(h, h, h, g.reshape(1, d), mu.reshape(2 * mu.shape[1], d), w_rkv.astype(BF16),
      w1c, w2p, w0, a1c, a2p, a0, g1.astype(BF16), g2.astype(BF16),
      k_k.reshape(1, d), k_a.reshape(1, d), r_k.reshape(1, d), hs, he)


def _wkv_kernel(r_ref, v_ref, kk_ref, lw_ref, kd_ref, bd_ref, y_ref, st_ref, *, ch):
    zdir = pl.program_id(1)
    c = pl.program_id(2)
    gw = WKV_GROUP_LANES
    gh = gw // HEAD_DIM
    n_sub = r_ref.shape[0] // ch
    n_grp = r_ref.shape[1] // gw
    fwd = zdir == 0

    @pl.when(c == 0)
    def _():
        st_ref[...] = jnp.zeros_like(st_ref)

    sign = jnp.where(fwd, 1, -1)
    tt = lax.broadcasted_iota(jnp.int32, (ch, gh * ch), 0)
    jl = lax.broadcasted_iota(jnp.int32, (ch, gh * ch), 1)
    jj = jl - (jl // ch) * ch
    order = (tt - jj) * sign
    strict = order > 0
    incl = order >= 0
    eye = (tt == jj).astype(F32)
    r8 = lax.broadcasted_iota(jnp.int32, (ch, ch), 0)
    c8 = lax.broadcasted_iota(jnp.int32, (ch, ch), 1)
    tri = jnp.where((r8 - c8) * sign >= 0, 1.0, 0.0).astype(BF16)
    same_head = (lax.broadcasted_iota(jnp.int32, (gh * ch, gw), 0) // ch
                 == lax.broadcasted_iota(jnp.int32, (gh * ch, gw), 1) // HEAD_DIM)
    same_head_sq = (lax.broadcasted_iota(jnp.int32, (gw, gw), 0) // HEAD_DIM
                    == lax.broadcasted_iota(jnp.int32, (gw, gw), 1) // HEAD_DIM)

    def bd(x):
        xb = x.astype(BF16)
        return jnp.where(same_head, jnp.concatenate([xb] * gh, axis=0), jnp.zeros((), BF16))

    grps = range(n_grp)
    sls = [slice(g * gw, (g + 1) * gw) for g in grps]
    rows, g_tot, lr, bk, bkh, vw = [], [], [], [], [], []
    for k in range(n_sub):
        start = pl.multiple_of(jnp.where(fwd, k * ch, (n_sub - 1 - k) * ch), ch)
        rk = pl.ds(start, ch)
        rows.append(rk)
        lw = lw_ref[rk, :]
        lw_hi = lw.astype(BF16)
        lw_lo = (lw - lw_hi.astype(F32)).astype(BF16)
        cum = _dot(tri, lw_hi) + _dot(tri, lw_lo)
        tot = jnp.where(fwd, cum[ch - 1:ch], cum[0:1])
        e_neg = jnp.exp(-cum)
        gk = jnp.exp(tot)
        r_t = r_ref[rk, :].astype(F32) * jnp.exp(cum)
        a_t = -(kk_ref[rk, :].astype(F32) * jnp.exp(cum - lw))
        k_t = kd_ref[rk, :].astype(F32) * e_neg
        b_t = bd_ref[rk, :].astype(F32) * e_neg
        k_h = k_t * gk
        b_h = b_t * gk
        v_k = v_ref[rk, :]
        g_tot.append(gk)
        lr.append([jnp.concatenate([a_t[:, sl], r_t[:, sl]], axis=0).astype(BF16) for sl in sls])
        bk.append([jnp.concatenate([bd(b_t[:, sl]), bd(k_t[:, sl])], axis=0) for sl in sls])
        bkh.append([jnp.concatenate([b_h[:, sl], k_h[:, sl]], axis=0).astype(BF16) for sl in sls])
        vw.append([v_k[:, sl] for sl in sls])

    chains = [(k, g) for k in range(n_sub) for g in grps]
    n_levels = int(math.log2(ch))
    wd = gh * ch
    sc = {kg: _dot_nt(lr[kg[0]][kg[1]], bk[kg[0]][kg[1]]) for kg in chains}
    lab = {kg: jnp.where(strict, sc[kg][:ch, :wd], 0.0) for kg in chains}
    lak = {kg: jnp.where(strict, sc[kg][:ch, wd:], 0.0).astype(BF16) for kg in chains}
    mr = {kg: jnp.concatenate([jnp.where(incl, sc[kg][ch:, :wd], 0.0).astype(BF16),
                               jnp.where(incl, sc[kg][ch:, wd:], 0.0).astype(BF16)], axis=1) for kg in chains}
    vbd = {kg: bd(vw[kg[0]][kg[1]]) for kg in chains}
    t_inv = {kg: eye + lab[kg] for kg in chains}
    pk = {kg: _dot(lab[kg].astype(BF16), bd(lab[kg])) for kg in chains}
    for lvl in range(1, n_levels):
        pb = {kg: bd(pk[kg]) for kg in chains}
        if lvl < n_levels - 1:
            tp = {kg: _dot(jnp.concatenate([t_inv[kg], pk[kg]], axis=0).astype(BF16), pb[kg]) for kg in chains}
            t_inv = {kg: t_inv[kg] + tp[kg][:ch] for kg in chains}
            pk = {kg: tp[kg][ch:] for kg in chains}
        else:
            t_inv = {kg: t_inv[kg] + _dot(t_inv[kg].astype(BF16), pb[kg]) for kg in chains}
    t_b = {kg: t_inv[kg].astype(BF16) for kg in chains}
    lv = {kg: _dot(lak[kg], vbd[kg]) for kg in chains}

    st = [st_ref[g] for g in grps]
    for k in range(n_sub):
        lrs = [_dot_nt(lr[k][g], st[g].astype(BF16)) for g in grps]
        u = [_dot(t_b[(k, g)], bd(lrs[g][:ch] + lv[(k, g)])) for g in grps]
        uvb = [jnp.concatenate([bd(u[g]), vbd[(k, g)]], axis=0) for g in grps]
        for g in grps:
            y_ref[rows[k], sls[g]] = (lrs[g][ch:] + _dot(mr[(k, g)], uvb[g])).astype(y_ref.dtype)
        uv = [jnp.concatenate([u[g].astype(BF16), vw[k][g]], axis=0) for g in grps]
        st = [st[g] * g_tot[k][:, sls[g]] + jnp.where(same_head_sq, _dot_tn(uv[g], bkh[k][g]), 0.0) for g in grps]
    for g in grps:
        st_ref[g] = st[g]


def _wkv(r, v, kkn, lw, kd, bd):
    b, s, d = r.shape
    ch = min(WKV_CHUNK, s)
    rows = min(WKV_CHUNKS_PER_STEP * ch, s)
    nb = s // rows
    assert ch == HEAD_DIM, "wide-tile masks assume chunk length == head_dim"
    n_grp = d // WKV_GROUP_LANES

    def bidx(zdir, c):
        return jnp.where(zdir == 0, c, nb - 1 - c)

    shared = pl.BlockSpec((None, rows, d), lambda bi, zdir, c: (bi, bidx(zdir, c), 0))
    perdir = pl.BlockSpec((None, None, rows, d), lambda bi, zdir, c: (zdir, bi, bidx(zdir, c), 0))
    return pl.pallas_call(
        functools.partial(_wkv_kernel, ch=ch),
        grid=(b, 2, nb),
        in_specs=[shared, shared, shared, perdir, perdir, perdir],
        out_specs=perdir,
        out_shape=jax.ShapeDtypeStruct((2, b, s, d), BF16),
        scratch_shapes=[pltpu.VMEM((n_grp, WKV_GROUP_LANES, WKV_GROUP_LANES), F32)],
        compiler_params=_params(("parallel", "parallel", "arbitrary")),
        name="wkv7_chunked_scan",
    )(r, v, kkn, lw, kd, bd)


def _rwkv_post_kernel(y_ref, bonus_ref, gate_ref, h_ref, lnw_ref, lnb_ref, hs_ref, hem_ref, wo_ref, o_ref):
    ts = h_ref.shape[0]
    rb = ts // POST_SUB_BLOCKS if ts % POST_SUB_BLOCKS == 0 else ts
    subs = [pl.ds(i * rb, rb) for i in range(ts // rb)]
    y = [y_ref[0, r, :] + y_ref[1, r, :] for r in subs]
    mean = [_head_reduce(v, hs_ref, hem_ref) for v in y]
    dlt = [v - m for v, m in zip(y, mean)]
    var = [_head_reduce(v * v, hs_ref, hem_ref) for v in dlt]
    lnw = lnw_ref[...]
    lnb = lnb_ref[...]
    out = [((dv * lax.rsqrt(vr + GN_EPS) * lnw + lnb + bonus_ref[r, :]) * gate_ref[r, :]).astype(BF16)
           for dv, vr, r in zip(dlt, var, subs)]
    for v, r in zip(out, subs):
        o_ref[r, :] = h_ref[r, :] + _dot(v, wo_ref[...])


def _rwkv_post(y, bonus, gate, h, ln_w, ln_b, hs, hem, w_o):
    b, s, d = h.shape
    ts = min(512, s)
    tile = pl.BlockSpec((None, ts, d), lambda bi, i: (bi, i, 0))
    vec = pl.BlockSpec((1, d), lambda bi, i: (0, 0))
    mat = pl.BlockSpec((d, d), lambda bi, i: (0, 0))
    return pl.pallas_call(
        _rwkv_post_kernel,
        grid=(b, s // ts),
        in_specs=[pl.BlockSpec((2, None, ts, d), lambda bi, i: (0, bi, i, 0)),
                  tile, tile, tile, vec, vec,
                  pl.BlockSpec(hs.shape, lambda bi, i: (0, 0)), pl.BlockSpec(hem.shape, lambda bi, i: (0, 0)), mat],
        out_specs=tile,
        out_shape=jax.ShapeDtypeStruct((b, s, d), F32),
        compiler_params=_params(("parallel", "arbitrary")),
        name="rwkv_groupnorm_proj",
    )(y, bonus, gate, h, ln_w.reshape(1, d), ln_b.reshape(1, d), hs, hem, w_o.astype(BF16))


def _rwkv_layer(h, g, mu, w_rkv, w0, w1, w2, a0, a1, a2, g1, g2, k_k, k_a, r_k, ln_w, ln_b, w_o):
    d = h.shape[-1]
    head_of = np.arange(d) // HEAD_DIM
    ind = (head_of[:, None] == np.arange(LANES)[None, :]).astype(np.float32)
    hs = jnp.asarray(ind).astype(BF16)
    he = jnp.asarray(ind.T).astype(BF16)
    hem = jnp.asarray(ind.T / HEAD_DIM).astype(BF16)
    r, v, kkn, lw, kd, bd, bonus, gate = _rwkv_pre(
        h, g, mu, w_rkv, w0, w1, w2, a0, a1, a2, g1, g2, k_k, k_a, r_k, hs, he)
    y = _wkv(r, v, kkn, lw, kd, bd)
    return _rwkv_post(y, bonus, gate, h, ln_w, ln_b, hs, hem, w_o)


def kernel(x, norm_mix_g, norm_ffn_g, norm_final_g, fnet_w_o, fnet_b_o, rwkv_mu, rwkv_w_rkv, rwkv_w0, rwkv_w1, rwkv_w2, rwkv_a0, rwkv_a1, rwkv_a2, rwkv_g1, rwkv_g2, rwkv_k_k, rwkv_k_a, rwkv_r_k, rwkv_ln_w, rwkv_ln_b, rwkv_w_o, ffn_w_in, ffn_conv_w, ffn_conv_b, ffn_w_out):
    depth = norm_mix_g.shape[0]
    h = x
    for i in range(depth):
        j = i // 2
        if i % 2 == 0:
            h = _fnet_layer(h, norm_mix_g[i], fnet_w_o[j], fnet_b_o[j])
        else:
            h = _rwkv_layer(h, norm_mix_g[i], rwkv_mu[j], rwkv_w_rkv[j], rwkv_w0[j], rwkv_w1[j],
                            rwkv_w2[j], rwkv_a0[j], rwkv_a1[j], rwkv_a2[j], rwkv_g1[j], rwkv_g2[j],
                            rwkv_k_k[j], rwkv_k_a[j], rwkv_r_k[j], rwkv_ln_w[j], rwkv_ln_b[j],
                            rwkv_w_o[j])
        g_final = norm_final_g if i == depth - 1 else None
        h = _convffn_layer(h, norm_ffn_g[i], ffn_w_in[i], ffn_conv_w[i], ffn_conv_b[i],
                           ffn_w_out[i], g_final, mirrored=(i % 2 == 0))
    return h
```

```python
import functools
import math

import jax
import jax.numpy as jnp
import numpy as np
from jax import lax
from jax.experimental import pallas as pl
from jax.experimental.pallas import tpu as pltpu

HEAD_DIM = 64
FNET_GROUPS = 4
RMS_EPS = 1e-6
GN_EPS = 64e-5
L2_EPS = 1e-12

LANES = 128
SUBLANES = 8
WKV_CHUNK = 64
WKV_CHUNKS_PER_STEP = 4
WKV_GROUP_LANES = 128
FNET_SEQ_ROWS = 512
FFN_ROW_BLOCK = 512
FFN_OUT_ROWS = 512
POST_SUB_BLOCKS = 2
PRE_SUB_BLOCKS = 2
PRE_ROWS = 512
VMEM_LIMIT_BYTES = 56 * 1024 * 1024

F32 = jnp.float32
BF16 = jnp.bfloat16


def _params(semantics):
    return pltpu.CompilerParams(dimension_semantics=semantics, vmem_limit_bytes=VMEM_LIMIT_BYTES)


def _rms(x, g):
    ms = jnp.mean(x * x, axis=-1, keepdims=True)
    return x * lax.rsqrt(ms + RMS_EPS) * g


def _dot(a, b):
    return jnp.dot(a, b, preferred_element_type=F32)


def _dot_nt(a, b):
    return lax.dot_general(a, b, (((1,), (1,)), ((), ())), preferred_element_type=F32)


def _dot_tn(a, b):
    return lax.dot_general(a, b, (((0,), (0,)), ((), ())), preferred_element_type=F32)


def _head_reduce(x, hs_ref, he_ref):
    part = _dot(x.astype(BF16), hs_ref[...])
    return _dot(part.astype(BF16), he_ref[...])


def _fnet_kernel(x_ref, g_ref, cs_ref, c_ref, s_ref, jrev_ref, wo_ref, bo_ref, o_ref, zc_ref, zs_ref, *, scale):
    i = pl.program_id(1)
    nb = pl.num_programs(1)
    ts = o_ref.shape[1]

    @pl.when(i == 0)
    def _():
        hn = _rms(x_ref[...], g_ref[...]).astype(BF16)
        gd = cs_ref.shape[0]
        for grp in range(hn.shape[1] // gd):
            sl = slice(grp * gd, (grp + 1) * gd)
            z = _dot(hn[:, sl], cs_ref[...])
            zc_ref[:, sl] = z[:, :gd].astype(BF16)
            zs_ref[:, sl] = z[:, gd:].astype(BF16)

    a = _dot(c_ref[...], zc_ref[...])
    b = _dot(s_ref[...], zs_ref[...])
    direct = ((a[:ts] - b[:ts]) * scale).astype(BF16)
    g = ((a + b) * scale).astype(BF16)
    mirror = _dot(jrev_ref[...], g[:ts])
    row = lax.broadcasted_iota(jnp.int32, (ts, 1), 0)
    mirror = jnp.where(row == 0, g[ts:ts + 1].astype(F32), mirror).astype(BF16)
    wo = wo_ref[...]
    lo = pl.multiple_of(i * ts, ts)
    hi = pl.multiple_of((2 * nb - 1 - i) * ts, ts)
    o_ref[0] = x_ref[pl.ds(lo, ts), :] + _dot(direct, wo) + bo_ref[...]
    o_ref[1] = x_ref[pl.ds(hi, ts), :] + _dot(mirror, wo) + bo_ref[...]


def _dft_mats(n):
    idx = np.arange(n)
    ang = (2.0 * np.pi / n) * ((idx[:, None] * idx[None, :]) % n)
    return np.cos(ang), np.sin(ang)


def _bf16_const(x):
    return jnp.asarray(x, F32).astype(BF16)


def _fnet_layer(x, g, w_o, b_o):
    b, s, d = x.shape
    gd = d // FNET_GROUPS
    cc, sc = _dft_mats(gd)
    cs = _bf16_const(np.concatenate([cc, sc], axis=1))
    half = s // 2
    ts = min(FNET_SEQ_ROWS, half)
    nb = half // ts
    halo = 2 * SUBLANES
    cseq, sseq = _dft_mats(s)
    cext = _bf16_const(np.stack([cseq[i * ts:i * ts + ts + halo] for i in range(nb)]))
    sext = _bf16_const(np.stack([sseq[i * ts:i * ts + ts + halo] for i in range(nb)]))
    jrev = np.zeros((ts, ts), np.float32)
    jrev[np.arange(1, ts), ts - np.arange(1, ts)] = 1.0
    scale = 1.0 / math.sqrt(s * gd)
    const = lambda shape: pl.BlockSpec(shape, lambda bi, i: (0,) * len(shape), pipeline_mode=pl.Buffered(1))
    return pl.pallas_call(
        functools.partial(_fnet_kernel, scale=scale),
        grid=(b, nb),
        in_specs=[pl.BlockSpec((None, s, d), lambda bi, i: (bi, 0, 0)),
                  const((1, d)), const((gd, 2 * gd)),
                  pl.BlockSpec((None, ts + halo, s), lambda bi, i: (i, 0, 0)),
                  pl.BlockSpec((None, ts + halo, s), lambda bi, i: (i, 0, 0)),
                  const((ts, ts)), const((d, d)), const((1, d))],
        out_specs=pl.BlockSpec((None, 2, ts, d), lambda bi, i: (bi, 0, i, 0)),
        out_shape=jax.ShapeDtypeStruct((b, 2, half, d), F32),
        scratch_shapes=[pltpu.VMEM((s, d), BF16), pltpu.VMEM((s, d), BF16)],
        compiler_params=_params(("parallel", "arbitrary")),
        name="fnet_dft_proj",
    )(x, g.reshape(1, d), cs, cext, sext, _bf16_const(jrev), w_o.astype(BF16), b_o.reshape(1, d))


def _ffn_in_kernel(h_ref, g_ref, wg_ref, wv_ref, cwg_ref, cbg_ref, sg_ref, uv_ref, hn_ref, ug_ref,
                   *, row_block, mirror_rows, n_chunks):
    t = pl.program_id(0)

    @pl.when(t == 0)
    def _():
        ug_ref[...] = jnp.zeros_like(ug_ref)

    @pl.when((t % n_chunks == 0) & (t < pl.num_programs(0) - 1))
    def _():
        if mirror_rows:
            half = h_ref.shape[1]
            nbh = half // mirror_rows
            hn_ref[pl.ds(0, half), :] = _rms(h_ref[0], g_ref[...]).astype(BF16)
            for q in range(nbh):
                src = pl.ds((nbh - 1 - q) * mirror_rows, mirror_rows)
                hn_ref[pl.ds(half + q * mirror_rows, mirror_rows), :] = _rms(h_ref[1, src, :], g_ref[...]).astype(BF16)
        else:
            hn_ref[...] = _rms(h_ref[...], g_ref[...]).astype(BF16)

    s, ck = ug_ref.shape
    rb = min(row_block, s)
    nb = s // rb
    zero_row = jnp.zeros((1, ck), F32)
    prev = [zero_row if b == 0 else ug_ref[pl.ds(b * rb - SUBLANES, SUBLANES), :][SUBLANES - 1:] for b in range(nb)]
    nxt = [zero_row if b == nb - 1 else ug_ref[pl.ds((b + 1) * rb, SUBLANES), :][:1] for b in range(nb)]
    row = lax.broadcasted_iota(jnp.int32, (rb, 1), 0)
    first = row == 0
    last = row == rb - 1
    wg = wg_ref[...].astype(BF16)
    wv = wv_ref[...].astype(BF16)
    cw = cwg_ref[...]
    for b in range(nb):
        rows = pl.ds(b * rb, rb)
        u = ug_ref[rows, :]
        up = jnp.where(first, prev[b], pltpu.roll(u, 1, 0))
        un = jnp.where(last, nxt[b], pltpu.roll(u, rb - 1, 0))
        gate = up * cw[0:1] + u * cw[1:2] + un * cw[2:3] + cbg_ref[...]
        sg_ref[rows, :] = (gate * jax.nn.sigmoid(gate)).astype(BF16)
        hb = hn_ref[rows, :]
        ug_ref[rows, :] = _dot(hb, wg)
        uv_ref[rows, :] = _dot(hb, wv).astype(BF16)


def _ffn_out_kernel(sg_ref, uv_ref, vp_ref, vn_ref, cw_ref, cb_ref, wo_ref, h_ref, gf_ref,
                    o_ref, act_ref, *, final_norm, ck):
    i = pl.program_id(1)
    tm, dff = uv_ref.shape
    hr = vp_ref.shape[0]
    at_start = i == 0
    at_end = i == pl.num_programs(1) - 1
    row = lax.broadcasted_iota(jnp.int32, (tm, 1), 0)
    first = row == 0
    last = row == tm - 1

    for c in range(dff // ck):
        cols = slice(c * ck, (c + 1) * ck)
        u = uv_ref[:, cols].astype(F32)
        prev_edge = jnp.where(at_start, 0.0, vp_ref[hr - 1:hr, cols].astype(F32))
        next_edge = jnp.where(at_end, 0.0, vn_ref[0:1, cols].astype(F32))
        up = jnp.where(first, prev_edge, pltpu.roll(u, 1, 0))
        un = jnp.where(last, next_edge, pltpu.roll(u, tm - 1, 0))
        cw = cw_ref[:, cols]
        val = up * cw[0:1] + u * cw[1:2] + un * cw[2:3] + cb_ref[:, cols]
        act_ref[:, cols] = (sg_ref[:, cols].astype(F32) * val).astype(BF16)
    o = h_ref[...] + _dot(act_ref[...], wo_ref[...])
    if final_norm:
        o = _rms(o, gf_ref[...])
    o_ref[...] = o


def _convffn_layer(h, g, w_in_all, layer, conv_w, conv_b, w_out, g_final=None, mirrored=False):
    if mirrored:
        b, _, half, d = h.shape
        s = 2 * half
        mts = min(FNET_SEQ_ROWS, half)
    else:
        b, s, d = h.shape
        mts = 0
    dff = w_out.shape[0]
    ck = 256 if dff % 256 == 0 else dff
    nj = dff // ck
    conv_b = conv_b.reshape(1, 2 * dff)
    n_items = b * nj
    mm = lambda t: jnp.minimum(t, n_items - 1)
    ew = lambda t: jnp.maximum(t - 1, 0)
    sg, uv = pl.pallas_call(
        functools.partial(_ffn_in_kernel, row_block=FFN_ROW_BLOCK, mirror_rows=mts, n_chunks=nj),
        grid=(n_items + 1,),
        in_specs=[pl.BlockSpec((None, 2, s // 2, d), lambda t: (mm(t) // nj, 0, 0, 0)) if mirrored
                  else pl.BlockSpec((None, s, d), lambda t: (mm(t) // nj, 0, 0)),
                  pl.BlockSpec((1, d), lambda t: (0, 0)),
                  pl.BlockSpec((None, d, ck), lambda t: (layer, 0, mm(t) % nj)),
                  pl.BlockSpec((None, d, ck), lambda t: (layer, 0, nj + mm(t) % nj)),
                  pl.BlockSpec((3, ck), lambda t: (0, ew(t) % nj)),
                  pl.BlockSpec((1, ck), lambda t: (0, ew(t) % nj))],
        out_specs=[pl.BlockSpec((None, s, ck), lambda t: (ew(t) // nj, 0, ew(t) % nj)),
                   pl.BlockSpec((None, s, ck), lambda t: (mm(t) // nj, 0, mm(t) % nj))],
        out_shape=[jax.ShapeDtypeStruct((b, s, dff), BF16), jax.ShapeDtypeStruct((b, s, dff), BF16)],
        scratch_shapes=[pltpu.VMEM((s, d), BF16), pltpu.VMEM((s, ck), F32)],
        compiler_params=_params(("arbitrary",)),
        name="convffn_in",
    )(h, g.reshape(1, d), w_in_all, w_in_all, conv_w, conv_b)
    final_norm = g_final is not None
    gf = (g_final if final_norm else g).reshape(1, d)
    tm = mts if mirrored else min(FFN_OUT_ROWS, s)
    nt = s // tm
    if mirrored:
        nbh = (s // 2) // tm
        h_spec = pl.BlockSpec((None, None, tm, d),
                              lambda bi, i: (bi, i // nbh, jnp.where(i < nbh, i, 2 * nbh - 1 - i), 0))
    else:
        h_spec = pl.BlockSpec((None, tm, d), lambda bi, i: (bi, i, 0))
    hr = 2 * SUBLANES
    tph = tm // hr
    tile = pl.BlockSpec((None, tm, dff), lambda bi, i: (bi, i, 0))
    halo_prev = pl.BlockSpec((None, hr, dff), lambda bi, i: (bi, jnp.maximum(i * tph - 1, 0), 0))
    halo_next = pl.BlockSpec((None, hr, dff), lambda bi, i: (bi, jnp.minimum((i + 1) * tph, s // hr - 1), 0))
    const = lambda shape: pl.BlockSpec(shape, lambda bi, i: (0,) * len(shape), pipeline_mode=pl.Buffered(1))
    return pl.pallas_call(
        functools.partial(_ffn_out_kernel, final_norm=final_norm, ck=ck),
        grid=(b, nt),
        in_specs=[tile, tile, halo_prev, halo_next,
                  pl.BlockSpec((3, dff), lambda bi, i: (0, 1), pipeline_mode=pl.Buffered(1)),
                  pl.BlockSpec((1, dff), lambda bi, i: (0, 1), pipeline_mode=pl.Buffered(1)),
                  const((dff, d)), h_spec, const((1, d))],
        out_specs=pl.BlockSpec((None, tm, d), lambda bi, i: (bi, i, 0)),
        out_shape=jax.ShapeDtypeStruct((b, s, d), F32),
        scratch_shapes=[pltpu.VMEM((tm, dff), BF16)],
        compiler_params=_params(("parallel", "arbitrary")),
        name="convffn_out_final" if final_norm else "convffn_out",
    )(sg, uv, uv, uv, conv_w, conv_b, w_out.astype(BF16), h, gf)


def _rwkv_pre_kernel(h_ref, hp_ref, hx_ref, g_ref, mu_ref, wrkv_ref, w1_ref, w2_ref, w0_ref,
                     a1_ref, a2_ref, a0_ref, g1_ref, g2_ref, kk_ref, ka_ref, rk_ref, hs_ref, he_ref,
                     r_out, v_out, kkn_out, lw_out, kd_out, bd_out, bonus_out, gate_out):
    i = pl.program_id(1)
    g = g_ref[...]
    hn = _rms(h_ref[...], g)
    ts = hn.shape[0]
    prev_edge = _rms(hp_ref[...], g)[SUBLANES - 1:SUBLANES]
    next_edge = _rms(hx_ref[...], g)[0:1]
    prev_edge = jnp.where(i == 0, 0.0, prev_edge)
    next_edge = jnp.where(i == pl.num_programs(1) - 1, 0.0, next_edge)
    row = lax.broadcasted_iota(jnp.int32, (ts, 1), 0)
    prev = jnp.where(row == 0, prev_edge, pltpu.roll(hn, 1, 0))
    nxt = jnp.where(row == ts - 1, next_edge, pltpu.roll(hn, ts - 1, 0))
    dp = prev - hn
    dn = nxt - hn
    mu = mu_ref[...]
    n_mix = mu.shape[0] // 2

    rb = ts // PRE_SUB_BLOCKS if ts % PRE_SUB_BLOCKS == 0 else ts
    subs = [slice(j * rb, (j + 1) * rb) for j in range(ts // rb)]

    def mix(m):
        return [(hn[sb] + dp[sb] * mu[m:m + 1] + dn[sb] * mu[n_mix + m:n_mix + m + 1]).astype(BF16) for sb in subs]

    r = [_dot(x, wrkv_ref[0]) for x in mix(0)]
    k = [_dot(x, wrkv_ref[1]) for x in mix(1)]
    v = [_dot(x, wrkv_ref[2]) for x in mix(2)]
    tw = [jnp.tanh(_dot(x, w1_ref[...])).astype(BF16) for x in mix(3)]
    ta = [_dot(x, a1_ref[...]).astype(BF16) for x in mix(4)]
    tg = [jax.nn.sigmoid(_dot(x, g1_ref[...])).astype(BF16) for x in mix(5)]
    gate = [_dot(x, g2_ref[...]) for x in tg]

    kraw = [x * kk_ref[...] for x in k]
    ssq = [_head_reduce(x * x, hs_ref, he_ref) for x in kraw]
    kkn = [x * lax.rsqrt(jnp.maximum(q, L2_EPS * L2_EPS)) for x, q in zip(kraw, ssq)]
    kka = [x * ka_ref[...] for x in k]
    kd_base = [x - y for x, y in zip(k, kka)]
    kd_sum = [jnp.zeros_like(x) for x in k]
    for zdir in range(2):
        xw = [w0_ref[zdir:zdir + 1] + _dot(x, w2_ref[zdir]) for x in tw]
        la = [_dot(x, a2_ref[zdir]) for x in ta]
        for j, sb in enumerate(subs):
            lw_out[zdir, sb, :] = jax.nn.sigmoid(xw[j]) * (-math.exp(-0.5) * math.log2(math.e))
            ag = jax.nn.sigmoid(a0_ref[zdir:zdir + 1] + la[j])
            kd = kd_base[j] + kka[j] * ag
            kd_sum[j] = kd_sum[j] + kd
            kd_out[zdir, sb, :] = kd.astype(BF16)
            bd_out[zdir, sb, :] = (kkn[j] * ag).astype(BF16)
    bsum = [_head_reduce(r[j] * kd_sum[j] * rk_ref[...], hs_ref, he_ref) for j in range(len(subs))]
    for j, sb in enumerate(subs):
        r_out[sb, :] = r[j].astype(BF16)
        v_out[sb, :] = v[j].astype(BF16)
        kkn_out[sb, :] = kkn[j].astype(BF16)
        bonus_out[sb, :] = (bsum[j] * v[j]).astype(bonus_out.dtype)
        gate_out[sb, :] = gate[j].astype(gate_out.dtype)


def _rwkv_pre(h, g, mu, w_rkv, w0, w1, w2, a0, a1, a2, g1, g2, k_k, k_a, r_k, hs, he):
    b, s, d = h.shape
    ts = min(PRE_ROWS, s)
    nt = s // ts
    nh8 = s // SUBLANES
    rdec = w1.shape[-1]
    raaa = a1.shape[-1]
    w1c = jnp.concatenate([w1[0], w1[1]], axis=1).astype(BF16)
    a1c = jnp.concatenate([a1[0], a1[1]], axis=1).astype(BF16)
    zw = jnp.zeros_like(w2[0])
    za = jnp.zeros_like(a2[0])
    w2p = jnp.stack([jnp.concatenate([w2[0], zw], 0), jnp.concatenate([zw, w2[1]], 0)]).astype(BF16)
    a2p = jnp.stack([jnp.concatenate([a2[0], za], 0), jnp.concatenate([za, a2[1]], 0)]).astype(BF16)
    full2 = lambda shape: pl.BlockSpec(shape, lambda bi, i: (0,) * len(shape), pipeline_mode=pl.Buffered(1))
    tile = pl.BlockSpec((None, ts, d), lambda bi, i: (bi, i, 0))
    tile2 = pl.BlockSpec((2, None, ts, d), lambda bi, i: (0, bi, i, 0))
    tpb = ts // SUBLANES
    in_specs = [
        tile,
        pl.BlockSpec((None, SUBLANES, d), lambda bi, i: (bi, jnp.maximum(i * tpb - 1, 0), 0)),
        pl.BlockSpec((None, SUBLANES, d), lambda bi, i: (bi, jnp.minimum((i + 1) * tpb, nh8 - 1), 0)),
        full2((1, d)), full2((2 * mu.shape[1], d)), full2((3, d, d)),
        full2((d, 2 * rdec)), full2((2, 2 * rdec, d)), full2((2, d)),
        full2((d, 2 * raaa)), full2((2, 2 * raaa, d)), full2((2, d)),
        full2(g1.shape), full2(g2.shape), full2((1, d)), full2((1, d)), full2((1, d)),
        full2(hs.shape), full2(he.shape),
    ]
    out_specs = [tile, tile, tile, tile2, tile2, tile2, tile, tile]
    bsd = (b, s, d)
    out_shape = [jax.ShapeDtypeStruct(bsd, BF16), jax.ShapeDtypeStruct(bsd, BF16),
                 jax.ShapeDtypeStruct(bsd, BF16), jax.ShapeDtypeStruct((2,) + bsd, F32),
                 jax.ShapeDtypeStruct((2,) + bsd, BF16), jax.ShapeDtypeStruct((2,) + bsd, BF16),
                 jax.ShapeDtypeStruct(bsd, BF16), jax.ShapeDtypeStruct(bsd, BF16)]
    return pl.pallas_call(
        _rwkv_pre_kernel,
        grid=(b, nt),
        in_specs=in_specs,
        out_specs=out_specs,
        out_shape=out_shape,
        compiler_params=_params(("parallel", "arbitrary")),
        name="rwkv_projections",
    )(h, h, h, g.reshape(1, d), mu.reshape(2 * mu.shape[1], d), w_rkv.astype(BF16),
      w1c, w2p, w0, a1c, a2p, a0, g1.astype(BF16), g2.astype(BF16),
      k_k.reshape(1, d), k_a.reshape(1, d), r_k.reshape(1, d), hs, he)


def _wkv_kernel(r_ref, v_ref, kk_ref, lw_ref, kd_ref, bd_ref, y_ref, st_ref, *, ch):
    zdir = pl.program_id(1)
    c = pl.program_id(2)
    gw = WKV_GROUP_LANES
    gh = gw // HEAD_DIM
    n_sub = r_ref.shape[0] // ch
    n_grp = r_ref.shape[1] // gw
    fwd = zdir == 0

    @pl.when(c == 0)
    def _():
        st_ref[...] = jnp.zeros_like(st_ref)

    sign = jnp.where(fwd, 1, -1)
    tt = lax.broadcasted_iota(jnp.int32, (ch, gh * ch), 0)
    jl = lax.broadcasted_iota(jnp.int32, (ch, gh * ch), 1)
    jj = jl - (jl // ch) * ch
    order = (tt - jj) * sign
    strict = order > 0
    incl = order >= 0
    eye = (tt == jj).astype(F32)
    r8 = lax.broadcasted_iota(jnp.int32, (ch, ch), 0)
    c8 = lax.broadcasted_iota(jnp.int32, (ch, ch), 1)
    tri = jnp.where((r8 - c8) * sign >= 0, 1.0, 0.0).astype(BF16)
    same_head = (lax.broadcasted_iota(jnp.int32, (gh * ch, gw), 0) // ch
                 == lax.broadcasted_iota(jnp.int32, (gh * ch, gw), 1) // HEAD_DIM)
    same_head_sq = (lax.broadcasted_iota(jnp.int32, (gw, gw), 0) // HEAD_DIM
                    == lax.broadcasted_iota(jnp.int32, (gw, gw), 1) // HEAD_DIM)

    def bd(x):
        xb = x.astype(BF16)
        return jnp.where(same_head, jnp.concatenate([xb] * gh, axis=0), jnp.zeros((), BF16))

    grps = range(n_grp)
    sls = [slice(g * gw, (g + 1) * gw) for g in grps]
    rows, g_tot, lr, bk, bkh, vw = [], [], [], [], [], []
    for k in range(n_sub):
        start = pl.multiple_of(jnp.where(fwd, k * ch, (n_sub - 1 - k) * ch), ch)
        rk = pl.ds(start, ch)
        rows.append(rk)
        lw = lw_ref[rk, :]
        lw_hi = lw.astype(BF16)
        lw_lo = (lw - lw_hi.astype(F32)).astype(BF16)
        cum = _dot(tri, lw_hi) + _dot(tri, lw_lo)
        tot = jnp.where(fwd, cum[ch - 1:ch], cum[0:1])
        e_neg = jnp.exp2(-cum)
        gk = jnp.exp2(tot)
        r_t = r_ref[rk, :].astype(F32) * jnp.exp2(cum)
        a_t = -(kk_ref[rk, :].astype(F32) * jnp.exp2(cum - lw))
        k_t = kd_ref[rk, :].astype(F32) * e_neg
        b_t = bd_ref[rk, :].astype(F32) * e_neg
        k_h = k_t * gk
        b_h = b_t * gk
        v_k = v_ref[rk, :]
        g_tot.append(gk)
        lr.append([jnp.concatenate([a_t[:, sl], r_t[:, sl]], axis=0).astype(BF16) for sl in sls])
        bk.append([jnp.concatenate([bd(b_t[:, sl]), bd(k_t[:, sl])], axis=0) for sl in sls])
        bkh.append([jnp.concatenate([b_h[:, sl], k_h[:, sl]], axis=0).astype(BF16) for sl in sls])
        vw.append([v_k[:, sl] for sl in sls])

    chains = [(k, g) for k in range(n_sub) for g in grps]
    n_levels = int(math.log2(ch))
    wd = gh * ch
    sc = {kg: _dot_nt(lr[kg[0]][kg[1]], bk[kg[0]][kg[1]]) for kg in chains}
    lab = {kg: jnp.where(strict, sc[kg][:ch, :wd], 0.0) for kg in chains}
    lak = {kg: jnp.where(strict, sc[kg][:ch, wd:], 0.0).astype(BF16) for kg in chains}
    mr = {kg: jnp.concatenate([jnp.where(incl, sc[kg][ch:, :wd], 0.0).astype(BF16),
                               jnp.where(incl, sc[kg][ch:, wd:], 0.0).astype(BF16)], axis=1) for kg in chains}
    vbd = {kg: bd(vw[kg[0]][kg[1]]) for kg in chains}
    t_inv = {kg: eye + lab[kg] for kg in chains}
    pk = {kg: _dot(lab[kg].astype(BF16), bd(lab[kg])) for kg in chains}
    for lvl in range(1, n_levels):
        pb = {kg: bd(pk[kg]) for kg in chains}
        if lvl < n_levels - 1:
            tp = {kg: _dot(jnp.concatenate([t_inv[kg], pk[kg]], axis=0).astype(BF16), pb[kg]) for kg in chains}
            t_inv = {kg: t_inv[kg] + tp[kg][:ch] for kg in chains}
            pk = {kg: tp[kg][ch:] for kg in chains}
        else:
            t_inv = {kg: t_inv[kg] + _dot(t_inv[kg].astype(BF16), pb[kg]) for kg in chains}
    t_b = {kg: t_inv[kg].astype(BF16) for kg in chains}
    lv = {kg: _dot(lak[kg], vbd[kg]) for kg in chains}

    st = [st_ref[g] for g in grps]
    for k in range(n_sub):
        lrs = [_dot_nt(lr[k][g], st[g].astype(BF16)) for g in grps]
        u = [_dot(t_b[(k, g)], bd(lrs[g][:ch] + lv[(k, g)])) for g in grps]
        uvb = [jnp.concatenate([bd(u[g]), vbd[(k, g)]], axis=0) for g in grps]
        for g in grps:
            y_ref[rows[k], sls[g]] = (lrs[g][ch:] + _dot(mr[(k, g)], uvb[g])).astype(y_ref.dtype)
        uv = [jnp.concatenate([u[g].astype(BF16), vw[k][g]], axis=0) for g in grps]
        st = [st[g] * g_tot[k][:, sls[g]] + jnp.where(same_head_sq, _dot_tn(uv[g], bkh[k][g]), 0.0) for g in grps]
    for g in grps:
        st_ref[g] = st[g]


def _wkv(r, v, kkn, lw, kd, bd):
    b, s, d = r.shape
    ch = min(WKV_CHUNK, s)
    rows = min(WKV_CHUNKS_PER_STEP * ch, s)
    nb = s // rows
    assert ch == HEAD_DIM, "wide-tile masks assume chunk length == head_dim"
    n_grp = d // WKV_GROUP_LANES

    def bidx(zdir, c):
        return jnp.where(zdir == 0, c, nb - 1 - c)

    shared = pl.BlockSpec((None, rows, d), lambda bi, zdir, c: (bi, bidx(zdir, c), 0))
    perdir = pl.BlockSpec((None, None, rows, d), lambda bi, zdir, c: (zdir, bi, bidx(zdir, c), 0))
    return pl.pallas_call(
        functools.partial(_wkv_kernel, ch=ch),
        grid=(b, 2, nb),
        in_specs=[shared, shared, shared, perdir, perdir, perdir],
        out_specs=perdir,
        out_shape=jax.ShapeDtypeStruct((2, b, s, d), BF16),
        scratch_shapes=[pltpu.VMEM((n_grp, WKV_GROUP_LANES, WKV_GROUP_LANES), F32)],
        compiler_params=_params(("parallel", "parallel", "arbitrary")),
        name="wkv7_chunked_scan",
    )(r, v, kkn, lw, kd, bd)


def _rwkv_post_kernel(y_ref, bonus_ref, gate_ref, h_ref, lnw_ref, lnb_ref, hs_ref, hem_ref, wo_ref, o_ref):
    ts = h_ref.shape[0]
    rb = ts // POST_SUB_BLOCKS if ts % POST_SUB_BLOCKS == 0 else ts
    subs = [pl.ds(i * rb, rb) for i in range(ts // rb)]
    y = [y_ref[0, r, :] + y_ref[1, r, :] for r in subs]
    mean = [_head_reduce(v, hs_ref, hem_ref) for v in y]
    dlt = [v - m for v, m in zip(y, mean)]
    var = [_head_reduce(v * v, hs_ref, hem_ref) for v in dlt]
    lnw = lnw_ref[...]
    lnb = lnb_ref[...]
    out = [((dv * lax.rsqrt(vr + GN_EPS) * lnw + lnb + bonus_ref[r, :]) * gate_ref[r, :]).astype(BF16)
           for dv, vr, r in zip(dlt, var, subs)]
    for v, r in zip(out, subs):
        o_ref[r, :] = h_ref[r, :] + _dot(v, wo_ref[...])


def _rwkv_post(y, bonus, gate, h, ln_w, ln_b, hs, hem, w_o):
    b, s, d = h.shape
    ts = min(512, s)
    tile = pl.BlockSpec((None, ts, d), lambda bi, i: (bi, i, 0))
    vec = pl.BlockSpec((1, d), lambda bi, i: (0, 0))
    mat = pl.BlockSpec((d, d), lambda bi, i: (0, 0))
    return pl.pallas_call(
        _rwkv_post_kernel,
        grid=(b, s // ts),
        in_specs=[pl.BlockSpec((2, None, ts, d), lambda bi, i: (0, bi, i, 0)),
                  tile, tile, tile, vec, vec,
                  pl.BlockSpec(hs.shape, lambda bi, i: (0, 0)), pl.BlockSpec(hem.shape, lambda bi, i: (0, 0)), mat],
        out_specs=tile,
        out_shape=jax.ShapeDtypeStruct((b, s, d), F32),
        compiler_params=_params(("parallel", "arbitrary")),
        name="rwkv_groupnorm_proj",
    )(y, bonus, gate, h, ln_w.reshape(1, d), ln_b.reshape(1, d), hs, hem, w_o.astype(BF16))


def _rwkv_layer(h, g, mu, w_rkv, w0, w1, w2, a0, a1, a2, g1, g2, k_k, k_a, r_k, ln_w, ln_b, w_o):
    d = h.shape[-1]
    head_of = np.arange(d) // HEAD_DIM
    ind = (head_of[:, None] == np.arange(LANES)[None, :]).astype(np.float32)
    hs = jnp.asarray(ind).astype(BF16)
    he = jnp.asarray(ind.T).astype(BF16)
    hem = jnp.asarray(ind.T / HEAD_DIM).astype(BF16)
    r, v, kkn, lw, kd, bd, bonus, gate = _rwkv_pre(
        h, g, mu, w_rkv, w0, w1, w2, a0, a1, a2, g1, g2, k_k, k_a, r_k, hs, he)
    y = _wkv(r, v, kkn, lw, kd, bd)
    return _rwkv_post(y, bonus, gate, h, ln_w, ln_b, hs, hem, w_o)


def kernel(x, norm_mix_g, norm_ffn_g, norm_final_g, fnet_w_o, fnet_b_o, rwkv_mu, rwkv_w_rkv, rwkv_w0, rwkv_w1, rwkv_w2, rwkv_a0, rwkv_a1, rwkv_a2, rwkv_g1, rwkv_g2, rwkv_k_k, rwkv_k_a, rwkv_r_k, rwkv_ln_w, rwkv_ln_b, rwkv_w_o, ffn_w_in, ffn_conv_w, ffn_conv_b, ffn_w_out):
    depth = norm_mix_g.shape[0]
    h = x
    for i in range(depth):
        j = i // 2
        if i % 2 == 0:
            h = _fnet_layer(h, norm_mix_g[i], fnet_w_o[j], fnet_b_o[j])
        else:
            h = _rwkv_layer(h, norm_mix_g[i], rwkv_mu[j], rwkv_w_rkv[j], rwkv_w0[j], rwkv_w1[j],
                            rwkv_w2[j], rwkv_a0[j], rwkv_a1[j], rwkv_a2[j], rwkv_g1[j], rwkv_g2[j],
                            rwkv_k_k[j], rwkv_k_a[j], rwkv_r_k[j], rwkv_ln_w[j], rwkv_ln_b[j],
                            rwkv_w_o[j])
        g_final = norm_final_g if i == depth - 1 else None
        h = _convffn_layer(h, norm_ffn_g[i], ffn_w_in, i, ffn_conv_w[i], ffn_conv_b[i],
                           ffn_w_out[i], g_final, mirrored=(i % 2 == 0))
    return h
```

```python
import functools
import math

import jax
import jax.numpy as jnp
import numpy as np
from jax import lax
from jax.experimental import pallas as pl
from jax.experimental.pallas import tpu as pltpu

HEAD_DIM = 64
FNET_GROUPS = 4
RMS_EPS = 1e-6
GN_EPS = 64e-5
L2_EPS = 1e-12

LANES = 128
SUBLANES = 8
WKV_CHUNK = 64
WKV_CHUNKS_PER_STEP = 4
WKV_GROUP_LANES = 128
FNET_SEQ_ROWS = 512
FFN_ROW_BLOCK = 512
FFN_OUT_ROWS = 512
POST_ROWS = 1024
POST_SUB_BLOCKS = 4
PRE_SUB_BLOCKS = 2
PRE_ROWS = 512
VMEM_LIMIT_BYTES = 56 * 1024 * 1024

F32 = jnp.float32
BF16 = jnp.bfloat16


def _params(semantics):
    return pltpu.CompilerParams(dimension_semantics=semantics, vmem_limit_bytes=VMEM_LIMIT_BYTES)


def _rms(x, g):
    ms = jnp.mean(x * x, axis=-1, keepdims=True)
    return x * lax.rsqrt(ms + RMS_EPS) * g


def _dot(a, b):
    return jnp.dot(a, b, preferred_element_type=F32)


def _dot_nt(a, b):
    return lax.dot_general(a, b, (((1,), (1,)), ((), ())), preferred_element_type=F32)


def _dot_tn(a, b):
    return lax.dot_general(a, b, (((0,), (0,)), ((), ())), preferred_element_type=F32)


def _head_reduce(x, hs_ref, he_ref):
    part = _dot(x.astype(BF16), hs_ref[...])
    return _dot(part.astype(BF16), he_ref[...])


def _fnet_kernel(x_ref, g_ref, cs_ref, c_ref, s_ref, jrev_ref, wo_ref, bo_ref, o_ref, zc_ref, zs_ref, *, scale):
    i = pl.program_id(1)
    nb = pl.num_programs(1)
    ts = o_ref.shape[1]

    @pl.when(i == 0)
    def _():
        hn = _rms(x_ref[...], g_ref[...]).astype(BF16)
        gd = cs_ref.shape[0]
        for grp in range(hn.shape[1] // gd):
            sl = slice(grp * gd, (grp + 1) * gd)
            z = _dot(hn[:, sl], cs_ref[...])
            zc_ref[:, sl] = z[:, :gd].astype(BF16)
            zs_ref[:, sl] = z[:, gd:].astype(BF16)

    a = _dot(c_ref[...], zc_ref[...])
    b = _dot(s_ref[...], zs_ref[...])
    direct = ((a[:ts] - b[:ts]) * scale).astype(BF16)
    g = ((a + b) * scale).astype(BF16)
    mirror = _dot(jrev_ref[...], g[:ts])
    row = lax.broadcasted_iota(jnp.int32, (ts, 1), 0)
    mirror = jnp.where(row == 0, g[ts:ts + 1].astype(F32), mirror).astype(BF16)
    wo = wo_ref[...]
    lo = pl.multiple_of(i * ts, ts)
    hi = pl.multiple_of((2 * nb - 1 - i) * ts, ts)
    o_ref[0] = x_ref[pl.ds(lo, ts), :] + _dot(direct, wo) + bo_ref[...]
    o_ref[1] = x_ref[pl.ds(hi, ts), :] + _dot(mirror, wo) + bo_ref[...]


def _dft_mats(n):
    idx = np.arange(n)
    ang = (2.0 * np.pi / n) * ((idx[:, None] * idx[None, :]) % n)
    return np.cos(ang), np.sin(ang)


def _bf16_const(x):
    return jnp.asarray(x, F32).astype(BF16)


def _fnet_layer(x, g, w_o, b_o):
    b, s, d = x.shape
    gd = d // FNET_GROUPS
    cc, sc = _dft_mats(gd)
    cs = _bf16_const(np.concatenate([cc, sc], axis=1))
    half = s // 2
    ts = min(FNET_SEQ_ROWS, half)
    nb = half // ts
    halo = 2 * SUBLANES
    cseq, sseq = _dft_mats(s)
    cext = _bf16_const(np.stack([cseq[i * ts:i * ts + ts + halo] for i in range(nb)]))
    sext = _bf16_const(np.stack([sseq[i * ts:i * ts + ts + halo] for i in range(nb)]))
    jrev = np.zeros((ts, ts), np.float32)
    jrev[np.arange(1, ts), ts - np.arange(1, ts)] = 1.0
    scale = 1.0 / math.sqrt(s * gd)
    const = lambda shape: pl.BlockSpec(shape, lambda bi, i: (0,) * len(shape), pipeline_mode=pl.Buffered(1))
    return pl.pallas_call(
        functools.partial(_fnet_kernel, scale=scale),
        grid=(b, nb),
        in_specs=[pl.BlockSpec((None, s, d), lambda bi, i: (bi, 0, 0)),
                  const((1, d)), const((gd, 2 * gd)),
                  pl.BlockSpec((None, ts + halo, s), lambda bi, i: (i, 0, 0)),
                  pl.BlockSpec((None, ts + halo, s), lambda bi, i: (i, 0, 0)),
                  const((ts, ts)), const((d, d)), const((1, d))],
        out_specs=pl.BlockSpec((None, 2, ts, d), lambda bi, i: (bi, 0, i, 0)),
        out_shape=jax.ShapeDtypeStruct((b, 2, half, d), F32),
        scratch_shapes=[pltpu.VMEM((s, d), BF16), pltpu.VMEM((s, d), BF16)],
        compiler_params=_params(("parallel", "arbitrary")),
        name="fnet_dft_proj",
    )(x, g.reshape(1, d), cs, cext, sext, _bf16_const(jrev), w_o.astype(BF16), b_o.reshape(1, d))


def _ffn_in_kernel(h_ref, g_ref, wg_ref, wv_ref, cwg_ref, cbg_ref, sg_ref, uv_ref, hn_ref, ug_ref,
                   *, row_block, mirror_rows, n_chunks):
    t = pl.program_id(0)

    @pl.when(t == 0)
    def _():
        ug_ref[...] = jnp.zeros_like(ug_ref)

    @pl.when((t % n_chunks == 0) & (t < pl.num_programs(0) - 1))
    def _():
        if mirror_rows:
            half = h_ref.shape[1]
            nbh = half // mirror_rows
            hn_ref[pl.ds(0, half), :] = _rms(h_ref[0], g_ref[...]).astype(BF16)
            for q in range(nbh):
                src = pl.ds((nbh - 1 - q) * mirror_rows, mirror_rows)
                hn_ref[pl.ds(half + q * mirror_rows, mirror_rows), :] = _rms(h_ref[1, src, :], g_ref[...]).astype(BF16)
        else:
            hn_ref[...] = _rms(h_ref[...], g_ref[...]).astype(BF16)

    s, ck = ug_ref.shape
    rb = min(row_block, s)
    nb = s // rb
    zero_row = jnp.zeros((1, ck), F32)
    prev = [zero_row if b == 0 else ug_ref[pl.ds(b * rb - SUBLANES, SUBLANES), :][SUBLANES - 1:] for b in range(nb)]
    nxt = [zero_row if b == nb - 1 else ug_ref[pl.ds((b + 1) * rb, SUBLANES), :][:1] for b in range(nb)]
    row = lax.broadcasted_iota(jnp.int32, (rb, 1), 0)
    first = row == 0
    last = row == rb - 1
    wg = wg_ref[...].astype(BF16)
    wv = wv_ref[...].astype(BF16)
    cw = cwg_ref[...]
    for b in range(nb):
        rows = pl.ds(b * rb, rb)
        u = ug_ref[rows, :]
        up = jnp.where(first, prev[b], pltpu.roll(u, 1, 0))
        un = jnp.where(last, nxt[b], pltpu.roll(u, rb - 1, 0))
        gate = up * cw[0:1] + u * cw[1:2] + un * cw[2:3] + cbg_ref[...]
        sg_ref[rows, :] = (gate * jax.nn.sigmoid(gate)).astype(BF16)
        hb = hn_ref[rows, :]
        ug_ref[rows, :] = _dot(hb, wg)
        uv_ref[rows, :] = _dot(hb, wv).astype(BF16)


def _ffn_out_kernel(sg_ref, uv_ref, vp_ref, vn_ref, cw_ref, cb_ref, wo_ref, h_ref, gf_ref,
                    o_ref, act_ref, *, final_norm, ck):
    i = pl.program_id(1)
    tm, dff = uv_ref.shape
    hr = vp_ref.shape[0]
    at_start = i == 0
    at_end = i == pl.num_programs(1) - 1
    row = lax.broadcasted_iota(jnp.int32, (tm, 1), 0)
    first = row == 0
    last = row == tm - 1

    for c in range(dff // ck):
        cols = slice(c * ck, (c + 1) * ck)
        u = uv_ref[:, cols].astype(F32)
        prev_edge = jnp.where(at_start, 0.0, vp_ref[hr - 1:hr, cols].astype(F32))
        next_edge = jnp.where(at_end, 0.0, vn_ref[0:1, cols].astype(F32))
        up = jnp.where(first, prev_edge, pltpu.roll(u, 1, 0))
        un = jnp.where(last, next_edge, pltpu.roll(u, tm - 1, 0))
        cw = cw_ref[:, cols]
        val = up * cw[0:1] + u * cw[1:2] + un * cw[2:3] + cb_ref[:, cols]
        act_ref[:, cols] = (sg_ref[:, cols].astype(F32) * val).astype(BF16)
    o = h_ref[...] + _dot(act_ref[...], wo_ref[...])
    if final_norm:
        o = _rms(o, gf_ref[...])
    o_ref[...] = o


def _convffn_layer(h, g, w_in_all, layer, conv_w, conv_b, w_out, g_final=None, mirrored=False):
    if mirrored:
        b, _, half, d = h.shape
        s = 2 * half
        mts = min(FNET_SEQ_ROWS, half)
    else:
        b, s, d = h.shape
        mts = 0
    dff = w_out.shape[0]
    ck = 256 if dff % 256 == 0 else dff
    nj = dff // ck
    conv_b = conv_b.reshape(1, 2 * dff)
    n_items = b * nj
    mm = lambda t: jnp.minimum(t, n_items - 1)
    ew = lambda t: jnp.maximum(t - 1, 0)
    sg, uv = pl.pallas_call(
        functools.partial(_ffn_in_kernel, row_block=FFN_ROW_BLOCK, mirror_rows=mts, n_chunks=nj),
        grid=(n_items + 1,),
        in_specs=[pl.BlockSpec((None, 2, s // 2, d), lambda t: (mm(t) // nj, 0, 0, 0)) if mirrored
                  else pl.BlockSpec((None, s, d), lambda t: (mm(t) // nj, 0, 0)),
                  pl.BlockSpec((1, d), lambda t: (0, 0)),
                  pl.BlockSpec((None, d, ck), lambda t: (layer, 0, mm(t) % nj)),
                  pl.BlockSpec((None, d, ck), lambda t: (layer, 0, nj + mm(t) % nj)),
                  pl.BlockSpec((3, ck), lambda t: (0, ew(t) % nj)),
                  pl.BlockSpec((1, ck), lambda t: (0, ew(t) % nj))],
        out_specs=[pl.BlockSpec((None, s, ck), lambda t: (ew(t) // nj, 0, ew(t) % nj)),
                   pl.BlockSpec((None, s, ck), lambda t: (mm(t) // nj, 0, mm(t) % nj))],
        out_shape=[jax.ShapeDtypeStruct((b, s, dff), BF16), jax.ShapeDtypeStruct((b, s, dff), BF16)],
        scratch_shapes=[pltpu.VMEM((s, d), BF16), pltpu.VMEM((s, ck), F32)],
        compiler_params=_params(("arbitrary",)),
        name="convffn_in",
    )(h, g.reshape(1, d), w_in_all, w_in_all, conv_w, conv_b)
    final_norm = g_final is not None
    gf = (g_final if final_norm else g).reshape(1, d)
    tm = mts if mirrored else min(FFN_OUT_ROWS, s)
    nt = s // tm
    if mirrored:
        nbh = (s // 2) // tm
        h_spec = pl.BlockSpec((None, None, tm, d),
                              lambda bi, i: (bi, i // nbh, jnp.where(i < nbh, i, 2 * nbh - 1 - i), 0))
    else:
        h_spec = pl.BlockSpec((None, tm, d), lambda bi, i: (bi, i, 0))
    hr = 2 * SUBLANES
    tph = tm // hr
    tile = pl.BlockSpec((None, tm, dff), lambda bi, i: (bi, i, 0))
    halo_prev = pl.BlockSpec((None, hr, dff), lambda bi, i: (bi, jnp.maximum(i * tph - 1, 0), 0))
    halo_next = pl.BlockSpec((None, hr, dff), lambda bi, i: (bi, jnp.minimum((i + 1) * tph, s // hr - 1), 0))
    const = lambda shape: pl.BlockSpec(shape, lambda bi, i: (0,) * len(shape), pipeline_mode=pl.Buffered(1))
    return pl.pallas_call(
        functools.partial(_ffn_out_kernel, final_norm=final_norm, ck=ck),
        grid=(b, nt),
        in_specs=[tile, tile, halo_prev, halo_next,
                  pl.BlockSpec((3, dff), lambda bi, i: (0, 1), pipeline_mode=pl.Buffered(1)),
                  pl.BlockSpec((1, dff), lambda bi, i: (0, 1), pipeline_mode=pl.Buffered(1)),
                  const((dff, d)), h_spec, const((1, d))],
        out_specs=pl.BlockSpec((None, tm, d), lambda bi, i: (bi, i, 0)),
        out_shape=jax.ShapeDtypeStruct((b, s, d), F32),
        scratch_shapes=[pltpu.VMEM((tm, dff), BF16)],
        compiler_params=_params(("parallel", "arbitrary")),
        name="convffn_out_final" if final_norm else "convffn_out",
    )(sg, uv, uv, uv, conv_w, conv_b, w_out.astype(BF16), h, gf)


def _rwkv_pre_kernel(h_ref, hp_ref, hx_ref, g_ref, mu_ref, wrkv_ref, w1_ref, w2_ref, w0_ref,
                     a1_ref, a2_ref, a0_ref, g1_ref, g2_ref, kk_ref, ka_ref, rk_ref, hs_ref, he_ref,
                     r_out, v_out, kkn_out, lw_out, kd_out, bd_out, bonus_out, gate_out):
    i = pl.program_id(1)
    g = g_ref[...]
    hn = _rms(h_ref[...], g)
    ts = hn.shape[0]
    prev_edge = _rms(hp_ref[...], g)[SUBLANES - 1:SUBLANES]
    next_edge = _rms(hx_ref[...], g)[0:1]
    prev_edge = jnp.where(i == 0, 0.0, prev_edge)
    next_edge = jnp.where(i == pl.num_programs(1) - 1, 0.0, next_edge)
    row = lax.broadcasted_iota(jnp.int32, (ts, 1), 0)
    prev = jnp.where(row == 0, prev_edge, pltpu.roll(hn, 1, 0))
    nxt = jnp.where(row == ts - 1, next_edge, pltpu.roll(hn, ts - 1, 0))
    dp = prev - hn
    dn = nxt - hn
    mu = mu_ref[...]
    n_mix = mu.shape[0] // 2

    rb = ts // PRE_SUB_BLOCKS if ts % PRE_SUB_BLOCKS == 0 else ts
    subs = [slice(j * rb, (j + 1) * rb) for j in range(ts // rb)]

    def mix(m):
        return [(hn[sb] + dp[sb] * mu[m:m + 1] + dn[sb] * mu[n_mix + m:n_mix + m + 1]).astype(BF16) for sb in subs]

    r = [_dot(x, wrkv_ref[0]) for x in mix(0)]
    k = [_dot(x, wrkv_ref[1]) for x in mix(1)]
    v = [_dot(x, wrkv_ref[2]) for x in mix(2)]
    tw = [jnp.tanh(_dot(x, w1_ref[...])).astype(BF16) for x in mix(3)]
    ta = [_dot(x, a1_ref[...]).astype(BF16) for x in mix(4)]
    tg = [jax.nn.sigmoid(_dot(x, g1_ref[...])).astype(BF16) for x in mix(5)]
    gate = [_dot(x, g2_ref[...]) for x in tg]

    kraw = [x * kk_ref[...] for x in k]
    ssq = [_head_reduce(x * x, hs_ref, he_ref) for x in kraw]
    kkn = [x * lax.rsqrt(jnp.maximum(q, L2_EPS * L2_EPS)) for x, q in zip(kraw, ssq)]
    kka = [x * ka_ref[...] for x in k]
    kd_base = [x - y for x, y in zip(k, kka)]
    kd_sum = [jnp.zeros_like(x) for x in k]
    for zdir in range(2):
        xw = [w0_ref[zdir:zdir + 1] + _dot(x, w2_ref[zdir]) for x in tw]
        la = [_dot(x, a2_ref[zdir]) for x in ta]
        for j, sb in enumerate(subs):
            lw_out[zdir, sb, :] = jax.nn.sigmoid(xw[j]) * (-math.exp(-0.5) * math.log2(math.e))
            ag = jax.nn.sigmoid(a0_ref[zdir:zdir + 1] + la[j])
            kd = kd_base[j] + kka[j] * ag
            kd_sum[j] = kd_sum[j] + kd
            kd_out[zdir, sb, :] = kd.astype(BF16)
            bd_out[zdir, sb, :] = (kkn[j] * ag).astype(BF16)
    bsum = [_head_reduce(r[j] * kd_sum[j] * rk_ref[...], hs_ref, he_ref) for j in range(len(subs))]
    for j, sb in enumerate(subs):
        r_out[sb, :] = r[j].astype(BF16)
        v_out[sb, :] = v[j].astype(BF16)
        kkn_out[sb, :] = kkn[j].astype(BF16)
        bonus_out[sb, :] = (bsum[j] * v[j]).astype(bonus_out.dtype)
        gate_out[sb, :] = gate[j].astype(gate_out.dtype)


def _rwkv_pre(h, g, mu, w_rkv, w0, w1, w2, a0, a1, a2, g1, g2, k_k, k_a, r_k, hs, he):
    b, s, d = h.shape
    ts = min(PRE_ROWS, s)
    nt = s // ts
    nh8 = s // SUBLANES
    rdec = w1.shape[-1]
    raaa = a1.shape[-1]
    w1c = jnp.concatenate([w1[0], w1[1]], axis=1).astype(BF16)
    a1c = jnp.concatenate([a1[0], a1[1]], axis=1).astype(BF16)
    zw = jnp.zeros_like(w2[0])
    za = jnp.zeros_like(a2[0])
    w2p = jnp.stack([jnp.concatenate([w2[0], zw], 0), jnp.concatenate([zw, w2[1]], 0)]).astype(BF16)
    a2p = jnp.stack([jnp.concatenate([a2[0], za], 0), jnp.concatenate([za, a2[1]], 0)]).astype(BF16)
    full2 = lambda shape: pl.BlockSpec(shape, lambda bi, i: (0,) * len(shape), pipeline_mode=pl.Buffered(1))
    tile = pl.BlockSpec((None, ts, d), lambda bi, i: (bi, i, 0))
    tile2 = pl.BlockSpec((2, None, ts, d), lambda bi, i: (0, bi, i, 0))
    tpb = ts // SUBLANES
    in_specs = [
        tile,
        pl.BlockSpec((None, SUBLANES, d), lambda bi, i: (bi, jnp.maximum(i * tpb - 1, 0), 0)),
        pl.BlockSpec((None, SUBLANES, d), lambda bi, i: (bi, jnp.minimum((i + 1) * tpb, nh8 - 1), 0)),
        full2((1, d)), full2((2 * mu.shape[1], d)), full2((3, d, d)),
        full2((d, 2 * rdec)), full2((2, 2 * rdec, d)), full2((2, d)),
        full2((d, 2 * raaa)), full2((2, 2 * raaa, d)), full2((2, d)),
        full2(g1.shape), full2(g2.shape), full2((1, d)), full2((1, d)), full2((1, d)),
        full2(hs.shape), full2(he.shape),
    ]
    out_specs = [tile, tile, tile, tile2, tile2, tile2, tile, tile]
    bsd = (b, s, d)
    out_shape = [jax.ShapeDtypeStruct(bsd, BF16), jax.ShapeDtypeStruct(bsd, BF16),
                 jax.ShapeDtypeStruct(bsd, BF16), jax.ShapeDtypeStruct((2,) + bsd, F32),
                 jax.ShapeDtypeStruct((2,) + bsd, BF16), jax.ShapeDtypeStruct((2,) + bsd, BF16),
                 jax.ShapeDtypeStruct(bsd, BF16), jax.ShapeDtypeStruct(bsd, BF16)]
    return pl.pallas_call(
        _rwkv_pre_kernel,
        grid=(b, nt),
        in_specs=in_specs,
        out_specs=out_specs,
        out_shape=out_shape,
        compiler_params=_params(("parallel", "arbitrary")),
        name="rwkv_projections",
    )(h, h, h, g.reshape(1, d), mu.reshape(2 * mu.shape[1], d), w_rkv.astype(BF16),
      w1c, w2p, w0, a1c, a2p, a0, g1.astype(BF16), g2.astype(BF16),
      k_k.reshape(1, d), k_a.reshape(1, d), r_k.reshape(1, d), hs, he)


def _wkv_kernel(r_ref, v_ref, kk_ref, lw_ref, kd_ref, bd_ref, y_ref, st_ref, *, ch):
    zdir = pl.program_id(1)
    c = pl.program_id(2)
    gw = WKV_GROUP_LANES
    gh = gw // HEAD_DIM
    n_sub = r_ref.shape[0] // ch
    n_grp = r_ref.shape[1] // gw
    fwd = zdir == 0

    @pl.when(c == 0)
    def _():
        st_ref[...] = jnp.zeros_like(st_ref)

    sign = jnp.where(fwd, 1, -1)
    tt = lax.broadcasted_iota(jnp.int32, (ch, gh * ch), 0)
    jl = lax.broadcasted_iota(jnp.int32, (ch, gh * ch), 1)
    jj = jl - (jl // ch) * ch
    order = (tt - jj) * sign
    strict = order > 0
    incl = order >= 0
    eye = (tt == jj).astype(F32)
    r8 = lax.broadcasted_iota(jnp.int32, (ch, ch), 0)
    c8 = lax.broadcasted_iota(jnp.int32, (ch, ch), 1)
    tri = jnp.where((r8 - c8) * sign >= 0, 1.0, 0.0).astype(BF16)
    same_head = (lax.broadcasted_iota(jnp.int32, (gh * ch, gw), 0) // ch
                 == lax.broadcasted_iota(jnp.int32, (gh * ch, gw), 1) // HEAD_DIM)
    same_head_sq = (lax.broadcasted_iota(jnp.int32, (gw, gw), 0) // HEAD_DIM
                    == lax.broadcasted_iota(jnp.int32, (gw, gw), 1) // HEAD_DIM)

    def bd(x):
        xb = x.astype(BF16)
        return jnp.where(same_head, jnp.concatenate([xb] * gh, axis=0), jnp.zeros((), BF16))

    grps = range(n_grp)
    sls = [slice(g * gw, (g + 1) * gw) for g in grps]
    rows, g_tot, lr, bk, bkh, vw = [], [], [], [], [], []
    for k in range(n_sub):
        start = pl.multiple_of(jnp.where(fwd, k * ch, (n_sub - 1 - k) * ch), ch)
        rk = pl.ds(start, ch)
        rows.append(rk)
        lw = lw_ref[rk, :]
        lw_hi = lw.astype(BF16)
        lw_lo = (lw - lw_hi.astype(F32)).astype(BF16)
        cum = _dot(tri, lw_hi) + _dot(tri, lw_lo)
        tot = jnp.where(fwd, cum[ch - 1:ch], cum[0:1])
        e_neg = jnp.exp2(-cum)
        gk = jnp.exp2(tot)
        r_t = r_ref[rk, :].astype(F32) * jnp.exp2(cum)
        a_t = -(kk_ref[rk, :].astype(F32) * jnp.exp2(cum - lw))
        k_t = kd_ref[rk, :].astype(F32) * e_neg
        b_t = bd_ref[rk, :].astype(F32) * e_neg
        k_h = k_t * gk
        b_h = b_t * gk
        v_k = v_ref[rk, :]
        g_tot.append(gk)
        lr.append([jnp.concatenate([a_t[:, sl], r_t[:, sl]], axis=0).astype(BF16) for sl in sls])
        bk.append([jnp.concatenate([bd(b_t[:, sl]), bd(k_t[:, sl])], axis=0) for sl in sls])
        bkh.append([jnp.concatenate([b_h[:, sl], k_h[:, sl]], axis=0).astype(BF16) for sl in sls])
        vw.append([v_k[:, sl] for sl in sls])

    chains = [(k, g) for k in range(n_sub) for g in grps]
    n_levels = int(math.log2(ch))
    wd = gh * ch
    sc = {kg: _dot_nt(lr[kg[0]][kg[1]], bk[kg[0]][kg[1]]) for kg in chains}
    lab = {kg: jnp.where(strict, sc[kg][:ch, :wd], 0.0) for kg in chains}
    lak = {kg: jnp.where(strict, sc[kg][:ch, wd:], 0.0).astype(BF16) for kg in chains}
    mr = {kg: jnp.concatenate([jnp.where(incl, sc[kg][ch:, :wd], 0.0).astype(BF16),
                               jnp.where(incl, sc[kg][ch:, wd:], 0.0).astype(BF16)], axis=1) for kg in chains}
    vbd = {kg: bd(vw[kg[0]][kg[1]]) for kg in chains}
    t_inv = {kg: eye + lab[kg] for kg in chains}
    pk = {kg: _dot(lab[kg].astype(BF16), bd(lab[kg])) for kg in chains}
    for lvl in range(1, n_levels):
        pb = {kg: bd(pk[kg]) for kg in chains}
        if lvl < n_levels - 1:
            tp = {kg: _dot(jnp.concatenate([t_inv[kg], pk[kg]], axis=0).astype(BF16), pb[kg]) for kg in chains}
            t_inv = {kg: t_inv[kg] + tp[kg][:ch] for kg in chains}
            pk = {kg: tp[kg][ch:] for kg in chains}
        else:
            t_inv = {kg: t_inv[kg] + _dot(t_inv[kg].astype(BF16), pb[kg]) for kg in chains}
    t_b = {kg: t_inv[kg].astype(BF16) for kg in chains}
    lv = {kg: _dot(lak[kg], vbd[kg]) for kg in chains}

    st = [st_ref[g] for g in grps]
    for k in range(n_sub):
        lrs = [_dot_nt(lr[k][g], st[g].astype(BF16)) for g in grps]
        u = [_dot(t_b[(k, g)], bd(lrs[g][:ch] + lv[(k, g)])) for g in grps]
        uvb = [jnp.concatenate([bd(u[g]), vbd[(k, g)]], axis=0) for g in grps]
        for g in grps:
            y_ref[rows[k], sls[g]] = (lrs[g][ch:] + _dot(mr[(k, g)], uvb[g])).astype(y_ref.dtype)
        uv = [jnp.concatenate([u[g].astype(BF16), vw[k][g]], axis=0) for g in grps]
        st = [st[g] * g_tot[k][:, sls[g]] + jnp.where(same_head_sq, _dot_tn(uv[g], bkh[k][g]), 0.0) for g in grps]
    for g in grps:
        st_ref[g] = st[g]


def _wkv(r, v, kkn, lw, kd, bd):
    b, s, d = r.shape
    ch = min(WKV_CHUNK, s)
    rows = min(WKV_CHUNKS_PER_STEP * ch, s)
    nb = s // rows
    assert ch == HEAD_DIM, "wide-tile masks assume chunk length == head_dim"
    n_grp = d // WKV_GROUP_LANES

    def bidx(zdir, c):
        return jnp.where(zdir == 0, c, nb - 1 - c)

    shared = pl.BlockSpec((None, rows, d), lambda bi, zdir, c: (bi, bidx(zdir, c), 0))
    perdir = pl.BlockSpec((None, None, rows, d), lambda bi, zdir, c: (zdir, bi, bidx(zdir, c), 0))
    return pl.pallas_call(
        functools.partial(_wkv_kernel, ch=ch),
        grid=(b, 2, nb),
        in_specs=[shared, shared, shared, perdir, perdir, perdir],
        out_specs=perdir,
        out_shape=jax.ShapeDtypeStruct((2, b, s, d), BF16),
        scratch_shapes=[pltpu.VMEM((n_grp, WKV_GROUP_LANES, WKV_GROUP_LANES), F32)],
        compiler_params=_params(("parallel", "parallel", "arbitrary")),
        name="wkv7_chunked_scan",
    )(r, v, kkn, lw, kd, bd)


def _rwkv_post_kernel(y_ref, bonus_ref, gate_ref, h_ref, lnw_ref, lnb_ref, hs_ref, hem_ref, wo_ref, o_ref):
    ts = h_ref.shape[0]
    rb = ts // POST_SUB_BLOCKS if ts % POST_SUB_BLOCKS == 0 else ts
    subs = [pl.ds(i * rb, rb) for i in range(ts // rb)]
    y = [y_ref[0, r, :] + y_ref[1, r, :] for r in subs]
    mean = [_head_reduce(v, hs_ref, hem_ref) for v in y]
    dlt = [v - m for v, m in zip(y, mean)]
    var = [_head_reduce(v * v, hs_ref, hem_ref) for v in dlt]
    lnw = lnw_ref[...]
    lnb = lnb_ref[...]
    out = [((dv * lax.rsqrt(vr + GN_EPS) * lnw + lnb + bonus_ref[r, :]) * gate_ref[r, :]).astype(BF16)
           for dv, vr, r in zip(dlt, var, subs)]
    for v, r in zip(out, subs):
        o_ref[r, :] = h_ref[r, :] + _dot(v, wo_ref[...])


def _rwkv_post(y, bonus, gate, h, ln_w, ln_b, hs, hem, w_o):
    b, s, d = h.shape
    ts = min(POST_ROWS, s)
    tile = pl.BlockSpec((None, ts, d), lambda bi, i: (bi, i, 0))
    vec = pl.BlockSpec((1, d), lambda bi, i: (0, 0))
    mat = pl.BlockSpec((d, d), lambda bi, i: (0, 0))
    return pl.pallas_call(
        _rwkv_post_kernel,
        grid=(b, s // ts),
        in_specs=[pl.BlockSpec((2, None, ts, d), lambda bi, i: (0, bi, i, 0)),
                  tile, tile, tile, vec, vec,
                  pl.BlockSpec(hs.shape, lambda bi, i: (0, 0)), pl.BlockSpec(hem.shape, lambda bi, i: (0, 0)), mat],
        out_specs=tile,
        out_shape=jax.ShapeDtypeStruct((b, s, d), F32),
        compiler_params=_params(("parallel", "arbitrary")),
        name="rwkv_groupnorm_proj",
    )(y, bonus, gate, h, ln_w.reshape(1, d), ln_b.reshape(1, d), hs, hem, w_o.astype(BF16))


def _rwkv_layer(h, g, mu, w_rkv, w0, w1, w2, a0, a1, a2, g1, g2, k_k, k_a, r_k, ln_w, ln_b, w_o):
    d = h.shape[-1]
    head_of = np.arange(d) // HEAD_DIM
    ind = (head_of[:, None] == np.arange(LANES)[None, :]).astype(np.float32)
    hs = jnp.asarray(ind).astype(BF16)
    he = jnp.asarray(ind.T).astype(BF16)
    hem = jnp.asarray(ind.T / HEAD_DIM).astype(BF16)
    r, v, kkn, lw, kd, bd, bonus, gate = _rwkv_pre(
        h, g, mu, w_rkv, w0, w1, w2, a0, a1, a2, g1, g2, k_k, k_a, r_k, hs, he)
    y = _wkv(r, v, kkn, lw, kd, bd)
    return _rwkv_post(y, bonus, gate, h, ln_w, ln_b, hs, hem, w_o)


def kernel(x, norm_mix_g, norm_ffn_g, norm_final_g, fnet_w_o, fnet_b_o, rwkv_mu, rwkv_w_rkv, rwkv_w0, rwkv_w1, rwkv_w2, rwkv_a0, rwkv_a1, rwkv_a2, rwkv_g1, rwkv_g2, rwkv_k_k, rwkv_k_a, rwkv_r_k, rwkv_ln_w, rwkv_ln_b, rwkv_w_o, ffn_w_in, ffn_conv_w, ffn_conv_b, ffn_w_out):
    depth = norm_mix_g.shape[0]
    h = x
    for i in range(depth):
        j = i // 2
        if i % 2 == 0:
            h = _fnet_layer(h, norm_mix_g[i], fnet_w_o[j], fnet_b_o[j])
        else:
            h = _rwkv_layer(h, norm_mix_g[i], rwkv_mu[j], rwkv_w_rkv[j], rwkv_w0[j], rwkv_w1[j],
                            rwkv_w2[j], rwkv_a0[j], rwkv_a1[j], rwkv_a2[j], rwkv_g1[j], rwkv_g2[j],
                            rwkv_k_k[j], rwkv_k_a[j], rwkv_r_k[j], rwkv_ln_w[j], rwkv_ln_b[j],
                            rwkv_w_o[j])
        g_final = norm_final_g if i == depth - 1 else None
        h = _convffn_layer(h, norm_ffn_g[i], ffn_w_in, i, ffn_conv_w[i], ffn_conv_b[i],
                           ffn_w_out[i], g_final, mirrored=(i % 2 == 0))
    return h
```

```python
import functools
import math

import jax
import jax.numpy as jnp
import numpy as np
from jax import lax
from jax.experimental import pallas as pl
from jax.experimental.pallas import tpu as pltpu

HEAD_DIM = 64
FNET_GROUPS = 4
RMS_EPS = 1e-6
GN_EPS = 64e-5
L2_EPS = 1e-12

LANES = 128
SUBLANES = 8
WKV_CHUNK = 64
WKV_CHUNKS_PER_STEP = 8
WKV_GROUP_LANES = LANES
FNET_SEQ_ROWS = 512
FFN_ROW_BLOCK = 512
FFN_OUT_ROWS = 512
POST_ROWS = 1024
POST_SUB_BLOCKS = 4
PRE_SUB_BLOCKS = 2
PRE_ROWS = 512
VMEM_LIMIT_BYTES = 56 * 1024 * 1024

F32 = jnp.float32
BF16 = jnp.bfloat16


def _params(semantics):
    return pltpu.CompilerParams(dimension_semantics=semantics, vmem_limit_bytes=VMEM_LIMIT_BYTES)


def _rms(x, g):
    ms = jnp.mean(x * x, axis=-1, keepdims=True)
    return x * lax.rsqrt(ms + RMS_EPS) * g


def _dot(a, b):
    return jnp.dot(a, b, preferred_element_type=F32)


def _dot_nt(a, b):
    return lax.dot_general(a, b, (((1,), (1,)), ((), ())), preferred_element_type=F32)


def _dot_tn(a, b):
    return lax.dot_general(a, b, (((0,), (0,)), ((), ())), preferred_element_type=F32)


def _head_reduce(x, hs_ref, he_ref):
    part = _dot(x.astype(BF16), hs_ref[...])
    return _dot(part.astype(BF16), he_ref[...])


def _fnet_kernel(x_ref, g_ref, cs_ref, c_ref, s_ref, jrev_ref, wo_ref, bo_ref, o_ref, zc_ref, zs_ref, *, scale):
    i = pl.program_id(1)
    nb = pl.num_programs(1)
    ts = o_ref.shape[1]

    @pl.when(i == 0)
    def _():
        hn = _rms(x_ref[...], g_ref[...]).astype(BF16)
        gd = cs_ref.shape[0]
        for grp in range(hn.shape[1] // gd):
            sl = slice(grp * gd, (grp + 1) * gd)
            z = _dot(hn[:, sl], cs_ref[...])
            zc_ref[:, sl] = z[:, :gd].astype(BF16)
            zs_ref[:, sl] = z[:, gd:].astype(BF16)

    a = _dot(c_ref[...], zc_ref[...])
    b = _dot(s_ref[...], zs_ref[...])
    direct = ((a[:ts] - b[:ts]) * scale).astype(BF16)
    g = ((a + b) * scale).astype(BF16)
    mirror = _dot(jrev_ref[...], g[:ts])
    row = lax.broadcasted_iota(jnp.int32, (ts, 1), 0)
    mirror = jnp.where(row == 0, g[ts:ts + 1].astype(F32), mirror).astype(BF16)
    wo = wo_ref[...]
    lo = pl.multiple_of(i * ts, ts)
    hi = pl.multiple_of((2 * nb - 1 - i) * ts, ts)
    o_ref[0] = x_ref[pl.ds(lo, ts), :] + _dot(direct, wo) + bo_ref[...]
    o_ref[1] = x_ref[pl.ds(hi, ts), :] + _dot(mirror, wo) + bo_ref[...]


def _dft_mats(n):
    idx = np.arange(n)
    ang = (2.0 * np.pi / n) * ((idx[:, None] * idx[None, :]) % n)
    return np.cos(ang), np.sin(ang)


def _bf16_const(x):
    return jnp.asarray(x, F32).astype(BF16)


def _fnet_layer(x, g, w_o, b_o):
    b, s, d = x.shape
    gd = d // FNET_GROUPS
    cc, sc = _dft_mats(gd)
    cs = _bf16_const(np.concatenate([cc, sc], axis=1))
    half = s // 2
    ts = min(FNET_SEQ_ROWS, half)
    nb = half // ts
    halo = 2 * SUBLANES
    cseq, sseq = _dft_mats(s)
    cext = _bf16_const(np.stack([cseq[i * ts:i * ts + ts + halo] for i in range(nb)]))
    sext = _bf16_const(np.stack([sseq[i * ts:i * ts + ts + halo] for i in range(nb)]))
    jrev = np.zeros((ts, ts), np.float32)
    jrev[np.arange(1, ts), ts - np.arange(1, ts)] = 1.0
    scale = 1.0 / math.sqrt(s * gd)
    const = lambda shape: pl.BlockSpec(shape, lambda bi, i: (0,) * len(shape), pipeline_mode=pl.Buffered(1))
    return pl.pallas_call(
        functools.partial(_fnet_kernel, scale=scale),
        grid=(b, nb),
        in_specs=[pl.BlockSpec((None, s, d), lambda bi, i: (bi, 0, 0)),
                  const((1, d)), const((gd, 2 * gd)),
                  pl.BlockSpec((None, ts + halo, s), lambda bi, i: (i, 0, 0)),
                  pl.BlockSpec((None, ts + halo, s), lambda bi, i: (i, 0, 0)),
                  const((ts, ts)), const((d, d)), const((1, d))],
        out_specs=pl.BlockSpec((None, 2, ts, d), lambda bi, i: (bi, 0, i, 0)),
        out_shape=jax.ShapeDtypeStruct((b, 2, half, d), F32),
        scratch_shapes=[pltpu.VMEM((s, d), BF16), pltpu.VMEM((s, d), BF16)],
        compiler_params=_params(("parallel", "arbitrary")),
        name="fnet_dft_proj",
    )(x, g.reshape(1, d), cs, cext, sext, _bf16_const(jrev), w_o.astype(BF16), b_o.reshape(1, d))


def _ffn_in_kernel(h_ref, g_ref, wg_ref, wv_ref, cwg_ref, cbg_ref, sg_ref, uv_ref, hn_ref, ug_ref,
                   *, row_block, mirror_rows, n_chunks):
    t = pl.program_id(0)

    @pl.when(t == 0)
    def _():
        ug_ref[...] = jnp.zeros_like(ug_ref)

    @pl.when((t % n_chunks == 0) & (t < pl.num_programs(0) - 1))
    def _():
        if mirror_rows:
            half = h_ref.shape[1]
            nbh = half // mirror_rows
            hn_ref[pl.ds(0, half), :] = _rms(h_ref[0], g_ref[...]).astype(BF16)
            for q in range(nbh):
                src = pl.ds((nbh - 1 - q) * mirror_rows, mirror_rows)
                hn_ref[pl.ds(half + q * mirror_rows, mirror_rows), :] = _rms(h_ref[1, src, :], g_ref[...]).astype(BF16)
        else:
            hn_ref[...] = _rms(h_ref[...], g_ref[...]).astype(BF16)

    s, ck = ug_ref.shape
    rb = min(row_block, s)
    nb = s // rb
    zero_row = jnp.zeros((1, ck), F32)
    prev = [zero_row if b == 0 else ug_ref[pl.ds(b * rb - SUBLANES, SUBLANES), :][SUBLANES - 1:] for b in range(nb)]
    nxt = [zero_row if b == nb - 1 else ug_ref[pl.ds((b + 1) * rb, SUBLANES), :][:1] for b in range(nb)]
    row = lax.broadcasted_iota(jnp.int32, (rb, 1), 0)
    first = row == 0
    last = row == rb - 1
    wg = wg_ref[...].astype(BF16)
    wv = wv_ref[...].astype(BF16)
    cw = cwg_ref[...]
    for b in range(nb):
        rows = pl.ds(b * rb, rb)
        u = ug_ref[rows, :]
        up = jnp.where(first, prev[b], pltpu.roll(u, 1, 0))
        un = jnp.where(last, nxt[b], pltpu.roll(u, rb - 1, 0))
        gate = up * cw[0:1] + u * cw[1:2] + un * cw[2:3] + cbg_ref[...]
        sg_ref[rows, :] = (gate * jax.nn.sigmoid(gate)).astype(BF16)
        hb = hn_ref[rows, :]
        ug_ref[rows, :] = _dot(hb, wg)
        uv_ref[rows, :] = _dot(hb, wv).astype(BF16)


def _ffn_out_kernel(sg_ref, uv_ref, vp_ref, vn_ref, cw_ref, cb_ref, wo_ref, h_ref, gf_ref,
                    o_ref, act_ref, *, final_norm, ck):
    i = pl.program_id(1)
    tm, dff = uv_ref.shape
    hr = vp_ref.shape[0]
    at_start = i == 0
    at_end = i == pl.num_programs(1) - 1
    row = lax.broadcasted_iota(jnp.int32, (tm, 1), 0)
    first = row == 0
    last = row == tm - 1

    for c in range(dff // ck):
        cols = slice(c * ck, (c + 1) * ck)
        u = uv_ref[:, cols].astype(F32)
        prev_edge = jnp.where(at_start, 0.0, vp_ref[hr - 1:hr, cols].astype(F32))
        next_edge = jnp.where(at_end, 0.0, vn_ref[0:1, cols].astype(F32))
        up = jnp.where(first, prev_edge, pltpu.roll(u, 1, 0))
        un = jnp.where(last, next_edge, pltpu.roll(u, tm - 1, 0))
        cw = cw_ref[:, cols]
        val = up * cw[0:1] + u * cw[1:2] + un * cw[2:3] + cb_ref[:, cols]
        act_ref[:, cols] = (sg_ref[:, cols].astype(F32) * val).astype(BF16)
    o = h_ref[...] + _dot(act_ref[...], wo_ref[...])
    if final_norm:
        o = _rms(o, gf_ref[...])
    o_ref[...] = o


def _convffn_layer(h, g, w_in_all, layer, conv_w, conv_b, w_out, g_final=None, mirrored=False):
    if mirrored:
        b, _, half, d = h.shape
        s = 2 * half
        mts = min(FNET_SEQ_ROWS, half)
    else:
        b, s, d = h.shape
        mts = 0
    dff = w_out.shape[0]
    ck = 256 if dff % 256 == 0 else dff
    nj = dff // ck
    conv_b = conv_b.reshape(1, 2 * dff)
    n_items = b * nj
    mm = lambda t: jnp.minimum(t, n_items - 1)
    ew = lambda t: jnp.maximum(t - 1, 0)
    sg, uv = pl.pallas_call(
        functools.partial(_ffn_in_kernel, row_block=FFN_ROW_BLOCK, mirror_rows=mts, n_chunks=nj),
        grid=(n_items + 1,),
        in_specs=[pl.BlockSpec((None, 2, s // 2, d), lambda t: (mm(t) // nj, 0, 0, 0)) if mirrored
                  else pl.BlockSpec((None, s, d), lambda t: (mm(t) // nj, 0, 0)),
                  pl.BlockSpec((1, d), lambda t: (0, 0)),
                  pl.BlockSpec((None, d, ck), lambda t: (layer, 0, mm(t) % nj)),
                  pl.BlockSpec((None, d, ck), lambda t: (layer, 0, nj + mm(t) % nj)),
                  pl.BlockSpec((3, ck), lambda t: (0, ew(t) % nj)),
                  pl.BlockSpec((1, ck), lambda t: (0, ew(t) % nj))],
        out_specs=[pl.BlockSpec((None, s, ck), lambda t: (ew(t) // nj, 0, ew(t) % nj)),
                   pl.BlockSpec((None, s, ck), lambda t: (mm(t) // nj, 0, mm(t) % nj))],
        out_shape=[jax.ShapeDtypeStruct((b, s, dff), BF16), jax.ShapeDtypeStruct((b, s, dff), BF16)],
        scratch_shapes=[pltpu.VMEM((s, d), BF16), pltpu.VMEM((s, ck), F32)],
        compiler_params=_params(("arbitrary",)),
        name="convffn_in",
    )(h, g.reshape(1, d), w_in_all, w_in_all, conv_w, conv_b)
    final_norm = g_final is not None
    gf = (g_final if final_norm else g).reshape(1, d)
    tm = mts if mirrored else min(FFN_OUT_ROWS, s)
    nt = s // tm
    if mirrored:
        nbh = (s // 2) // tm
        h_spec = pl.BlockSpec((None, None, tm, d),
                              lambda bi, i: (bi, i // nbh, jnp.where(i < nbh, i, 2 * nbh - 1 - i), 0))
    else:
        h_spec = pl.BlockSpec((None, tm, d), lambda bi, i: (bi, i, 0))
    hr = 2 * SUBLANES
    tph = tm // hr
    tile = pl.BlockSpec((None, tm, dff), lambda bi, i: (bi, i, 0))
    halo_prev = pl.BlockSpec((None, hr, dff), lambda bi, i: (bi, jnp.maximum(i * tph - 1, 0), 0))
    halo_next = pl.BlockSpec((None, hr, dff), lambda bi, i: (bi, jnp.minimum((i + 1) * tph, s // hr - 1), 0))
    const = lambda shape: pl.BlockSpec(shape, lambda bi, i: (0,) * len(shape), pipeline_mode=pl.Buffered(1))
    return pl.pallas_call(
        functools.partial(_ffn_out_kernel, final_norm=final_norm, ck=ck),
        grid=(b, nt),
        in_specs=[tile, tile, halo_prev, halo_next,
                  pl.BlockSpec((3, dff), lambda bi, i: (0, 1), pipeline_mode=pl.Buffered(1)),
                  pl.BlockSpec((1, dff), lambda bi, i: (0, 1), pipeline_mode=pl.Buffered(1)),
                  const((dff, d)), h_spec, const((1, d))],
        out_specs=pl.BlockSpec((None, tm, d), lambda bi, i: (bi, i, 0)),
        out_shape=jax.ShapeDtypeStruct((b, s, d), F32),
        scratch_shapes=[pltpu.VMEM((tm, dff), BF16)],
        compiler_params=_params(("parallel", "arbitrary")),
        name="convffn_out_final" if final_norm else "convffn_out",
    )(sg, uv, uv, uv, conv_w, conv_b, w_out.astype(BF16), h, gf)


def _rwkv_pre_kernel(h_ref, hp_ref, hx_ref, g_ref, mu_ref, wrkv_ref, w1_ref, w2_ref, w0_ref,
                     a1_ref, a2_ref, a0_ref, g1_ref, g2_ref, kk_ref, ka_ref, rk_ref, hs_ref, he_ref,
                     r_out, v_out, kkn_out, lw_out, kd_out, bd_out, bonus_out, gate_out):
    i = pl.program_id(1)
    g = g_ref[...]
    hn = _rms(h_ref[...], g)
    ts = hn.shape[0]
    prev_edge = _rms(hp_ref[...], g)[SUBLANES - 1:SUBLANES]
    next_edge = _rms(hx_ref[...], g)[0:1]
    prev_edge = jnp.where(i == 0, 0.0, prev_edge)
    next_edge = jnp.where(i == pl.num_programs(1) - 1, 0.0, next_edge)
    row = lax.broadcasted_iota(jnp.int32, (ts, 1), 0)
    prev = jnp.where(row == 0, prev_edge, pltpu.roll(hn, 1, 0))
    nxt = jnp.where(row == ts - 1, next_edge, pltpu.roll(hn, ts - 1, 0))
    dp = prev - hn
    dn = nxt - hn
    mu = mu_ref[...]
    n_mix = mu.shape[0] // 2

    rb = ts // PRE_SUB_BLOCKS if ts % PRE_SUB_BLOCKS == 0 else ts
    subs = [slice(j * rb, (j + 1) * rb) for j in range(ts // rb)]

    def mix(m):
        return [(hn[sb] + dp[sb] * mu[m:m + 1] + dn[sb] * mu[n_mix + m:n_mix + m + 1]).astype(BF16) for sb in subs]

    r = [_dot(x, wrkv_ref[0]) for x in mix(0)]
    k = [_dot(x, wrkv_ref[1]) for x in mix(1)]
    v = [_dot(x, wrkv_ref[2]) for x in mix(2)]
    tw = [jnp.tanh(_dot(x, w1_ref[...])).astype(BF16) for x in mix(3)]
    ta = [_dot(x, a1_ref[...]).astype(BF16) for x in mix(4)]
    tg = [jax.nn.sigmoid(_dot(x, g1_ref[...])).astype(BF16) for x in mix(5)]
    gate = [_dot(x, g2_ref[...]) for x in tg]

    kraw = [x * kk_ref[...] for x in k]
    ssq = [_head_reduce(x * x, hs_ref, he_ref) for x in kraw]
    kkn = [x * lax.rsqrt(jnp.maximum(q, L2_EPS * L2_EPS)) for x, q in zip(kraw, ssq)]
    kka = [x * ka_ref[...] for x in k]
    kd_base = [x - y for x, y in zip(k, kka)]
    kd_sum = [jnp.zeros_like(x) for x in k]
    for zdir in range(2):
        xw = [w0_ref[zdir:zdir + 1] + _dot(x, w2_ref[zdir]) for x in tw]
        la = [_dot(x, a2_ref[zdir]) for x in ta]
        for j, sb in enumerate(subs):
            lw_out[zdir, sb, :] = jax.nn.sigmoid(xw[j]) * (-math.exp(-0.5) * math.log2(math.e))
            ag = jax.nn.sigmoid(a0_ref[zdir:zdir + 1] + la[j])
            kd = kd_base[j] + kka[j] * ag
            kd_sum[j] = kd_sum[j] + kd
            kd_out[zdir, sb, :] = kd.astype(BF16)
            bd_out[zdir, sb, :] = (kkn[j] * ag).astype(BF16)
    bsum = [_head_reduce(r[j] * kd_sum[j] * rk_ref[...], hs_ref, he_ref) for j in range(len(subs))]
    for j, sb in enumerate(subs):
        r_out[sb, :] = r[j].astype(BF16)
        v_out[sb, :] = v[j].astype(BF16)
        kkn_out[sb, :] = kkn[j].astype(BF16)
        bonus_out[sb, :] = (bsum[j] * v[j]).astype(bonus_out.dtype)
        gate_out[sb, :] = gate[j].astype(gate_out.dtype)


def _rwkv_pre(h, g, mu, w_rkv, w0, w1, w2, a0, a1, a2, g1, g2, k_k, k_a, r_k, hs, he):
    b, s, d = h.shape
    ts = min(PRE_ROWS, s)
    nt = s // ts
    nh8 = s // SUBLANES
    rdec = w1.shape[-1]
    raaa = a1.shape[-1]
    w1c = jnp.concatenate([w1[0], w1[1]], axis=1).astype(BF16)
    a1c = jnp.concatenate([a1[0], a1[1]], axis=1).astype(BF16)
    zw = jnp.zeros_like(w2[0])
    za = jnp.zeros_like(a2[0])
    w2p = jnp.stack([jnp.concatenate([w2[0], zw], 0), jnp.concatenate([zw, w2[1]], 0)]).astype(BF16)
    a2p = jnp.stack([jnp.concatenate([a2[0], za], 0), jnp.concatenate([za, a2[1]], 0)]).astype(BF16)
    full2 = lambda shape: pl.BlockSpec(shape, lambda bi, i: (0,) * len(shape), pipeline_mode=pl.Buffered(1))
    tile = pl.BlockSpec((None, ts, d), lambda bi, i: (bi, i, 0))
    tile2 = pl.BlockSpec((2, None, ts, d), lambda bi, i: (0, bi, i, 0))
    tpb = ts // SUBLANES
    in_specs = [
        tile,
        pl.BlockSpec((None, SUBLANES, d), lambda bi, i: (bi, jnp.maximum(i * tpb - 1, 0), 0)),
        pl.BlockSpec((None, SUBLANES, d), lambda bi, i: (bi, jnp.minimum((i + 1) * tpb, nh8 - 1), 0)),
        full2((1, d)), full2((2 * mu.shape[1], d)), full2((3, d, d)),
        full2((d, 2 * rdec)), full2((2, 2 * rdec, d)), full2((2, d)),
        full2((d, 2 * raaa)), full2((2, 2 * raaa, d)), full2((2, d)),
        full2(g1.shape), full2(g2.shape), full2((1, d)), full2((1, d)), full2((1, d)),
        full2(hs.shape), full2(he.shape),
    ]
    out_specs = [tile, tile, tile, tile2, tile2, tile2, tile, tile]
    bsd = (b, s, d)
    out_shape = [jax.ShapeDtypeStruct(bsd, BF16), jax.ShapeDtypeStruct(bsd, BF16),
                 jax.ShapeDtypeStruct(bsd, BF16), jax.ShapeDtypeStruct((2,) + bsd, F32),
                 jax.ShapeDtypeStruct((2,) + bsd, BF16), jax.ShapeDtypeStruct((2,) + bsd, BF16),
                 jax.ShapeDtypeStruct(bsd, BF16), jax.ShapeDtypeStruct(bsd, BF16)]
    return pl.pallas_call(
        _rwkv_pre_kernel,
        grid=(b, nt),
        in_specs=in_specs,
        out_specs=out_specs,
        out_shape=out_shape,
        compiler_params=_params(("parallel", "arbitrary")),
        name="rwkv_projections",
    )(h, h, h, g.reshape(1, d), mu.reshape(2 * mu.shape[1], d), w_rkv.astype(BF16),
      w1c, w2p, w0, a1c, a2p, a0, g1.astype(BF16), g2.astype(BF16),
      k_k.reshape(1, d), k_a.reshape(1, d), r_k.reshape(1, d), hs, he)


def _wkv_kernel(r_ref, v_ref, kk_ref, lw_ref, kd_ref, bd_ref, y_ref, st_ref, *, ch):
    zdir = pl.program_id(1)
    c = pl.program_id(2)
    gw = WKV_GROUP_LANES
    gh = gw // HEAD_DIM
    n_sub = r_ref.shape[0] // ch
    n_grp = r_ref.shape[1] // gw
    fwd = zdir == 0

    @pl.when(c == 0)
    def _():
        st_ref[...] = jnp.zeros_like(st_ref)

    sign = jnp.where(fwd, 1, -1)
    tt = lax.broadcasted_iota(jnp.int32, (ch, gh * ch), 0)
    jl = lax.broadcasted_iota(jnp.int32, (ch, gh * ch), 1)
    jj = jl - (jl // ch) * ch
    order = (tt - jj) * sign
    strict = order > 0
    incl = order >= 0
    eye = (tt == jj).astype(F32)
    r8 = lax.broadcasted_iota(jnp.int32, (ch, ch), 0)
    c8 = lax.broadcasted_iota(jnp.int32, (ch, ch), 1)
    tri = jnp.where((r8 - c8) * sign >= 0, 1.0, 0.0).astype(BF16)
    same_head = (lax.broadcasted_iota(jnp.int32, (gh * ch, gw), 0) // ch
                 == lax.broadcasted_iota(jnp.int32, (gh * ch, gw), 1) // HEAD_DIM)
    same_head_sq = (lax.broadcasted_iota(jnp.int32, (gw, gw), 0) // HEAD_DIM
                    == lax.broadcasted_iota(jnp.int32, (gw, gw), 1) // HEAD_DIM)

    def bd(x):
        xb = x.astype(BF16)
        return jnp.where(same_head, jnp.concatenate([xb] * gh, axis=0), jnp.zeros((), BF16))

    grps = range(n_grp)
    sls = [slice(g * gw, (g + 1) * gw) for g in grps]
    rows, g_tot, lr, bk, bkh, vw = [], [], [], [], [], []
    for k in range(n_sub):
        start = pl.multiple_of(jnp.where(fwd, k * ch, (n_sub - 1 - k) * ch), ch)
        rk = pl.ds(start, ch)
        rows.append(rk)
        lw = lw_ref[rk, :]
        lw_hi = lw.astype(BF16)
        lw_lo = (lw - lw_hi.astype(F32)).astype(BF16)
        cum = _dot(tri, lw_hi) + _dot(tri, lw_lo)
        tot = jnp.where(fwd, cum[ch - 1:ch], cum[0:1])
        e_neg = jnp.exp2(-cum)
        gk = jnp.exp2(tot)
        r_t = r_ref[rk, :].astype(F32) * jnp.exp2(cum)
        a_t = -(kk_ref[rk, :].astype(F32) * jnp.exp2(cum - lw))
        k_t = kd_ref[rk, :].astype(F32) * e_neg
        b_t = bd_ref[rk, :].astype(F32) * e_neg
        v_k = v_ref[rk, :]
        g_tot.append(gk)
        lr.append([jnp.concatenate([a_t[:, sl], r_t[:, sl]], axis=0).astype(BF16) for sl in sls])
        bk.append([jnp.concatenate([bd(b_t[:, sl]), bd(k_t[:, sl])], axis=0) for sl in sls])
        bkh.append([jnp.concatenate([b_t[:, sl], k_t[:, sl]], axis=0).astype(BF16) for sl in sls])
        vw.append([v_k[:, sl] for sl in sls])

    chains = [(k, g) for k in range(n_sub) for g in grps]
    n_levels = int(math.log2(ch))
    wd = gh * ch
    sc = {kg: _dot_nt(lr[kg[0]][kg[1]], bk[kg[0]][kg[1]]) for kg in chains}
    lab = {kg: jnp.where(strict, sc[kg][:ch, :wd], 0.0) for kg in chains}
    lak = {kg: jnp.where(strict, sc[kg][:ch, wd:], 0.0).astype(BF16) for kg in chains}
    mr = {kg: jnp.concatenate([jnp.where(incl, sc[kg][ch:, :wd], 0.0).astype(BF16),
                               jnp.where(incl, sc[kg][ch:, wd:], 0.0).astype(BF16)], axis=1) for kg in chains}
    vbd = {kg: bd(vw[kg[0]][kg[1]]) for kg in chains}
    t_inv = {kg: eye + lab[kg] for kg in chains}
    pk = {kg: _dot(lab[kg].astype(BF16), bd(lab[kg])) for kg in chains}
    for lvl in range(1, n_levels):
        pb = {kg: bd(pk[kg]) for kg in chains}
        if lvl < n_levels - 1:
            tp = {kg: _dot(jnp.concatenate([t_inv[kg], pk[kg]], axis=0).astype(BF16), pb[kg]) for kg in chains}
            t_inv = {kg: t_inv[kg] + tp[kg][:ch] for kg in chains}
            pk = {kg: tp[kg][ch:] for kg in chains}
        else:
            t_inv = {kg: t_inv[kg] + _dot(t_inv[kg].astype(BF16), pb[kg]) for kg in chains}
    t_b = {kg: t_inv[kg].astype(BF16) for kg in chains}
    lv = {kg: _dot(lak[kg], vbd[kg]) for kg in chains}

    st = [st_ref[g] for g in grps]
    for k in range(n_sub):
        lrs = [_dot_nt(lr[k][g], st[g].astype(BF16)) for g in grps]
        u = [_dot(t_b[(k, g)], bd(lrs[g][:ch] + lv[(k, g)])) for g in grps]
        uvb = [jnp.concatenate([bd(u[g]), vbd[(k, g)]], axis=0) for g in grps]
        for g in grps:
            y_ref[rows[k], sls[g]] = (lrs[g][ch:] + _dot(mr[(k, g)], uvb[g])).astype(y_ref.dtype)
        uv = [jnp.concatenate([u[g].astype(BF16), vw[k][g]], axis=0) for g in grps]
        st = [(st[g] + jnp.where(same_head_sq, _dot_tn(uv[g], bkh[k][g]), 0.0)) * g_tot[k][:, sls[g]] for g in grps]
    for g in grps:
        st_ref[g] = st[g]


def _wkv(r, v, kkn, lw, kd, bd):
    b, s, d = r.shape
    ch = min(WKV_CHUNK, s)
    rows = min(WKV_CHUNKS_PER_STEP * ch, s)
    nb = s // rows
    assert ch == HEAD_DIM, "wide-tile masks assume chunk length == head_dim"
    n_grp = d // WKV_GROUP_LANES

    def bidx(zdir, c):
        return jnp.where(zdir == 0, c, nb - 1 - c)

    shared = pl.BlockSpec((None, rows, d), lambda bi, zdir, c: (bi, bidx(zdir, c), 0))
    perdir = pl.BlockSpec((None, None, rows, d), lambda bi, zdir, c: (zdir, bi, bidx(zdir, c), 0))
    return pl.pallas_call(
        functools.partial(_wkv_kernel, ch=ch),
        grid=(b, 2, nb),
        in_specs=[shared, shared, shared, perdir, perdir, perdir],
        out_specs=perdir,
        out_shape=jax.ShapeDtypeStruct((2, b, s, d), BF16),
        scratch_shapes=[pltpu.VMEM((n_grp, WKV_GROUP_LANES, WKV_GROUP_LANES), F32)],
        compiler_params=_params(("parallel", "parallel", "arbitrary")),
        name="wkv7_chunked_scan",
    )(r, v, kkn, lw, kd, bd)


def _rwkv_post_kernel(y_ref, bonus_ref, gate_ref, h_ref, lnw_ref, lnb_ref, hs_ref, hem_ref, wo_ref, o_ref):
    ts = h_ref.shape[0]
    rb = ts // POST_SUB_BLOCKS if ts % POST_SUB_BLOCKS == 0 else ts
    subs = [pl.ds(i * rb, rb) for i in range(ts // rb)]
    y = [y_ref[0, r, :] + y_ref[1, r, :] for r in subs]
    mean = [_head_reduce(v, hs_ref, hem_ref) for v in y]
    dlt = [v - m for v, m in zip(y, mean)]
    var = [_head_reduce(v * v, hs_ref, hem_ref) for v in dlt]
    lnw = lnw_ref[...]
    lnb = lnb_ref[...]
    out = [((dv * lax.rsqrt(vr + GN_EPS) * lnw + lnb + bonus_ref[r, :]) * gate_ref[r, :]).astype(BF16)
           for dv, vr, r in zip(dlt, var, subs)]
    for v, r in zip(out, subs):
        o_ref[r, :] = h_ref[r, :] + _dot(v, wo_ref[...])


def _rwkv_post(y, bonus, gate, h, ln_w, ln_b, hs, hem, w_o):
    b, s, d = h.shape
    ts = min(POST_ROWS, s)
    tile = pl.BlockSpec((None, ts, d), lambda bi, i: (bi, i, 0))
    vec = pl.BlockSpec((1, d), lambda bi, i: (0, 0))
    mat = pl.BlockSpec((d, d), lambda bi, i: (0, 0))
    return pl.pallas_call(
        _rwkv_post_kernel,
        grid=(b, s // ts),
        in_specs=[pl.BlockSpec((2, None, ts, d), lambda bi, i: (0, bi, i, 0)),
                  tile, tile, tile, vec, vec,
                  pl.BlockSpec(hs.shape, lambda bi, i: (0, 0)), pl.BlockSpec(hem.shape, lambda bi, i: (0, 0)), mat],
        out_specs=tile,
        out_shape=jax.ShapeDtypeStruct((b, s, d), F32),
        compiler_params=_params(("parallel", "arbitrary")),
        name="rwkv_groupnorm_proj",
    )(y, bonus, gate, h, ln_w.reshape(1, d), ln_b.reshape(1, d), hs, hem, w_o.astype(BF16))


def _rwkv_layer(h, g, mu, w_rkv, w0, w1, w2, a0, a1, a2, g1, g2, k_k, k_a, r_k, ln_w, ln_b, w_o):
    d = h.shape[-1]
    head_of = np.arange(d) // HEAD_DIM
    ind = (head_of[:, None] == np.arange(LANES)[None, :]).astype(np.float32)
    hs = jnp.asarray(ind).astype(BF16)
    he = jnp.asarray(ind.T).astype(BF16)
    hem = jnp.asarray(ind.T / HEAD_DIM).astype(BF16)
    r, v, kkn, lw, kd, bd, bonus, gate = _rwkv_pre(
        h, g, mu, w_rkv, w0, w1, w2, a0, a1, a2, g1, g2, k_k, k_a, r_k, hs, he)
    y = _wkv(r, v, kkn, lw, kd, bd)
    return _rwkv_post(y, bonus, gate, h, ln_w, ln_b, hs, hem, w_o)


def kernel(x, norm_mix_g, norm_ffn_g, norm_final_g, fnet_w_o, fnet_b_o, rwkv_mu, rwkv_w_rkv, rwkv_w0, rwkv_w1, rwkv_w2, rwkv_a0, rwkv_a1, rwkv_a2, rwkv_g1, rwkv_g2, rwkv_k_k, rwkv_k_a, rwkv_r_k, rwkv_ln_w, rwkv_ln_b, rwkv_w_o, ffn_w_in, ffn_conv_w, ffn_conv_b, ffn_w_out):
    depth = norm_mix_g.shape[0]
    h = x
    for i in range(depth):
        j = i // 2
        if i % 2 == 0:
            h = _fnet_layer(h, norm_mix_g[i], fnet_w_o[j], fnet_b_o[j])
        else:
            h = _rwkv_layer(h, norm_mix_g[i], rwkv_mu[j], rwkv_w_rkv[j], rwkv_w0[j], rwkv_w1[j],
                            rwkv_w2[j], rwkv_a0[j], rwkv_a1[j], rwkv_a2[j], rwkv_g1[j], rwkv_g2[j],
                            rwkv_k_k[j], rwkv_k_a[j], rwkv_r_k[j], rwkv_ln_w[j], rwkv_ln_b[j],
                            rwkv_w_o[j])
        g_final = norm_final_g if i == depth - 1 else None
        h = _convffn_layer(h, norm_ffn_g[i], ffn_w_in, i, ffn_conv_w[i], ffn_conv_b[i],
                           ffn_w_out[i], g_final, mirrored=(i % 2 == 0))
    return h
```

```python
import functools
import math

import jax
import jax.numpy as jnp
import numpy as np
from jax import lax
from jax.experimental import pallas as pl
from jax.experimental.pallas import tpu as pltpu

HEAD_DIM = 64
FNET_GROUPS = 4
RMS_EPS = 1e-6
GN_EPS = 64e-5
L2_EPS = 1e-12

LANES = 128
SUBLANES = 8
WKV_CHUNK = 64
WKV_CHUNKS_PER_STEP = 8
WKV_GROUP_LANES = LANES
FNET_SEQ_ROWS = 512
FFN_ROW_BLOCK = 512
FFN_OUT_ROWS = 512
POST_ROWS = 1024
POST_SUB_BLOCKS = 4
PRE_SUB_BLOCKS = 2
PRE_ROWS = 512
VMEM_LIMIT_BYTES = 56 * 1024 * 1024

F32 = jnp.float32
BF16 = jnp.bfloat16


def _params(semantics):
    return pltpu.CompilerParams(dimension_semantics=semantics, vmem_limit_bytes=VMEM_LIMIT_BYTES)


def _rms(x, g):
    ms = jnp.mean(x * x, axis=-1, keepdims=True)
    return x * lax.rsqrt(ms + RMS_EPS) * g


def _dot(a, b):
    return jnp.dot(a, b, preferred_element_type=F32)


def _dot_nt(a, b):
    return lax.dot_general(a, b, (((1,), (1,)), ((), ())), preferred_element_type=F32)


def _dot_tn(a, b):
    return lax.dot_general(a, b, (((0,), (0,)), ((), ())), preferred_element_type=F32)


def _head_reduce(x, hs_ref, he_ref):
    part = _dot(x.astype(BF16), hs_ref[...])
    return _dot(part.astype(BF16), he_ref[...])


def _fnet_kernel(x_ref, g_ref, cs_ref, c_ref, s_ref, jrev_ref, wo_ref, bo_ref, o_ref, zc_ref, zs_ref, *, scale):
    i = pl.program_id(1)
    nb = pl.num_programs(1)
    ts = o_ref.shape[1]

    @pl.when(i == 0)
    def _():
        hn = _rms(x_ref[...], g_ref[...]).astype(BF16)
        gd = cs_ref.shape[0]
        for grp in range(hn.shape[1] // gd):
            sl = slice(grp * gd, (grp + 1) * gd)
            z = _dot(hn[:, sl], cs_ref[...])
            zc_ref[:, sl] = z[:, :gd].astype(BF16)
            zs_ref[:, sl] = z[:, gd:].astype(BF16)

    a = _dot(c_ref[...], zc_ref[...])
    b = _dot(s_ref[...], zs_ref[...])
    direct = ((a[:ts] - b[:ts]) * scale).astype(BF16)
    g = ((a + b) * scale).astype(BF16)
    mirror = _dot(jrev_ref[...], g[:ts])
    row = lax.broadcasted_iota(jnp.int32, (ts, 1), 0)
    mirror = jnp.where(row == 0, g[ts:ts + 1].astype(F32), mirror).astype(BF16)
    wo = wo_ref[...]
    lo = pl.multiple_of(i * ts, ts)
    hi = pl.multiple_of((2 * nb - 1 - i) * ts, ts)
    o_ref[0] = x_ref[pl.ds(lo, ts), :] + _dot(direct, wo) + bo_ref[...]
    o_ref[1] = x_ref[pl.ds(hi, ts), :] + _dot(mirror, wo) + bo_ref[...]


def _dft_mats(n):
    idx = np.arange(n)
    ang = (2.0 * np.pi / n) * ((idx[:, None] * idx[None, :]) % n)
    return np.cos(ang), np.sin(ang)


def _bf16_const(x):
    return jnp.asarray(x, F32).astype(BF16)


def _fnet_layer(x, g, w_o, b_o):
    b, s, d = x.shape
    gd = d // FNET_GROUPS
    cc, sc = _dft_mats(gd)
    cs = _bf16_const(np.concatenate([cc, sc], axis=1))
    half = s // 2
    ts = min(FNET_SEQ_ROWS, half)
    nb = half // ts
    halo = 2 * SUBLANES
    cseq, sseq = _dft_mats(s)
    cext = _bf16_const(np.stack([cseq[i * ts:i * ts + ts + halo] for i in range(nb)]))
    sext = _bf16_const(np.stack([sseq[i * ts:i * ts + ts + halo] for i in range(nb)]))
    jrev = np.zeros((ts, ts), np.float32)
    jrev[np.arange(1, ts), ts - np.arange(1, ts)] = 1.0
    scale = 1.0 / math.sqrt(s * gd)
    const = lambda shape: pl.BlockSpec(shape, lambda bi, i: (0,) * len(shape), pipeline_mode=pl.Buffered(1))
    return pl.pallas_call(
        functools.partial(_fnet_kernel, scale=scale),
        grid=(b, nb),
        in_specs=[pl.BlockSpec((None, s, d), lambda bi, i: (bi, 0, 0)),
                  const((1, d)), const((gd, 2 * gd)),
                  pl.BlockSpec((None, ts + halo, s), lambda bi, i: (i, 0, 0)),
                  pl.BlockSpec((None, ts + halo, s), lambda bi, i: (i, 0, 0)),
                  const((ts, ts)), const((d, d)), const((1, d))],
        out_specs=pl.BlockSpec((None, 2, ts, d), lambda bi, i: (bi, 0, i, 0)),
        out_shape=jax.ShapeDtypeStruct((b, 2, half, d), F32),
        scratch_shapes=[pltpu.VMEM((s, d), BF16), pltpu.VMEM((s, d), BF16)],
        compiler_params=_params(("parallel", "arbitrary")),
        name="fnet_dft_proj",
    )(x, g.reshape(1, d), cs, cext, sext, _bf16_const(jrev), w_o.astype(BF16), b_o.reshape(1, d))


def _ffn_in_kernel(h_ref, g_ref, wg_ref, wv_ref, cwg_ref, cbg_ref, sg_ref, uv_ref, hn_ref, ug_ref,
                   *, row_block, mirror_rows, n_chunks):
    t = pl.program_id(0)

    @pl.when(t == 0)
    def _():
        ug_ref[...] = jnp.zeros_like(ug_ref)

    @pl.when((t % n_chunks == 0) & (t < pl.num_programs(0) - 1))
    def _():
        if mirror_rows:
            half = h_ref.shape[1]
            nbh = half // mirror_rows
            hn_ref[pl.ds(0, half), :] = _rms(h_ref[0], g_ref[...]).astype(BF16)
            for q in range(nbh):
                src = pl.ds((nbh - 1 - q) * mirror_rows, mirror_rows)
                hn_ref[pl.ds(half + q * mirror_rows, mirror_rows), :] = _rms(h_ref[1, src, :], g_ref[...]).astype(BF16)
        else:
            hn_ref[...] = _rms(h_ref[...], g_ref[...]).astype(BF16)

    s, ck = ug_ref.shape
    rb = min(row_block, s)
    nb = s // rb
    zero_row = jnp.zeros((1, ck), F32)
    prev = [zero_row if b == 0 else ug_ref[pl.ds(b * rb - SUBLANES, SUBLANES), :][SUBLANES - 1:] for b in range(nb)]
    nxt = [zero_row if b == nb - 1 else ug_ref[pl.ds((b + 1) * rb, SUBLANES), :][:1] for b in range(nb)]
    row = lax.broadcasted_iota(jnp.int32, (rb, 1), 0)
    first = row == 0
    last = row == rb - 1
    wg = wg_ref[...].astype(BF16)
    wv = wv_ref[...].astype(BF16)
    cw = cwg_ref[...]
    for b in range(nb):
        rows = pl.ds(b * rb, rb)
        u = ug_ref[rows, :]
        up = jnp.where(first, prev[b], pltpu.roll(u, 1, 0))
        un = jnp.where(last, nxt[b], pltpu.roll(u, rb - 1, 0))
        gate = up * cw[0:1] + u * cw[1:2] + un * cw[2:3] + cbg_ref[...]
        sg_ref[rows, :] = (gate * jax.nn.sigmoid(gate)).astype(BF16)
        hb = hn_ref[rows, :]
        ug_ref[rows, :] = _dot(hb, wg)
        uv_ref[rows, :] = _dot(hb, wv).astype(BF16)


def _ffn_out_kernel(sg_ref, uv_ref, vp_ref, vn_ref, cw_ref, cb_ref, wo_ref, h_ref, gf_ref,
                    o_ref, act_ref, *, final_norm, ck):
    i = pl.program_id(1)
    tm, dff = uv_ref.shape
    hr = vp_ref.shape[0]
    at_start = i == 0
    at_end = i == pl.num_programs(1) - 1
    row = lax.broadcasted_iota(jnp.int32, (tm, 1), 0)
    first = row == 0
    last = row == tm - 1

    for c in range(dff // ck):
        cols = slice(c * ck, (c + 1) * ck)
        u = uv_ref[:, cols].astype(F32)
        prev_edge = jnp.where(at_start, 0.0, vp_ref[hr - 1:hr, cols].astype(F32))
        next_edge = jnp.where(at_end, 0.0, vn_ref[0:1, cols].astype(F32))
        up = jnp.where(first, prev_edge, pltpu.roll(u, 1, 0))
        un = jnp.where(last, next_edge, pltpu.roll(u, tm - 1, 0))
        cw = cw_ref[:, cols]
        val = up * cw[0:1] + u * cw[1:2] + un * cw[2:3] + cb_ref[:, cols]
        act_ref[:, cols] = sg_ref[:, cols] * val.astype(BF16)
    o = h_ref[...] + _dot(act_ref[...], wo_ref[...])
    if final_norm:
        o = _rms(o, gf_ref[...])
    o_ref[...] = o


def _convffn_layer(h, g, w_in_all, layer, conv_w, conv_b, w_out, g_final=None, mirrored=False):
    if mirrored:
        b, _, half, d = h.shape
        s = 2 * half
        mts = min(FNET_SEQ_ROWS, half)
    else:
        b, s, d = h.shape
        mts = 0
    dff = w_out.shape[0]
    ck = 256 if dff % 256 == 0 else dff
    nj = dff // ck
    conv_b = conv_b.reshape(1, 2 * dff)
    n_items = b * nj
    mm = lambda t: jnp.minimum(t, n_items - 1)
    ew = lambda t: jnp.maximum(t - 1, 0)
    sg, uv = pl.pallas_call(
        functools.partial(_ffn_in_kernel, row_block=FFN_ROW_BLOCK, mirror_rows=mts, n_chunks=nj),
        grid=(n_items + 1,),
        in_specs=[pl.BlockSpec((None, 2, s // 2, d), lambda t: (mm(t) // nj, 0, 0, 0)) if mirrored
                  else pl.BlockSpec((None, s, d), lambda t: (mm(t) // nj, 0, 0)),
                  pl.BlockSpec((1, d), lambda t: (0, 0)),
                  pl.BlockSpec((None, d, ck), lambda t: (layer, 0, mm(t) % nj)),
                  pl.BlockSpec((None, d, ck), lambda t: (layer, 0, nj + mm(t) % nj)),
                  pl.BlockSpec((3, ck), lambda t: (0, ew(t) % nj)),
                  pl.BlockSpec((1, ck), lambda t: (0, ew(t) % nj))],
        out_specs=[pl.BlockSpec((None, s, ck), lambda t: (ew(t) // nj, 0, ew(t) % nj)),
                   pl.BlockSpec((None, s, ck), lambda t: (mm(t) // nj, 0, mm(t) % nj))],
        out_shape=[jax.ShapeDtypeStruct((b, s, dff), BF16), jax.ShapeDtypeStruct((b, s, dff), BF16)],
        scratch_shapes=[pltpu.VMEM((s, d), BF16), pltpu.VMEM((s, ck), F32)],
        compiler_params=_params(("arbitrary",)),
        name="convffn_in",
    )(h, g.reshape(1, d), w_in_all, w_in_all, conv_w, conv_b)
    final_norm = g_final is not None
    gf = (g_final if final_norm else g).reshape(1, d)
    tm = mts if mirrored else min(FFN_OUT_ROWS, s)
    nt = s // tm
    if mirrored:
        nbh = (s // 2) // tm
        h_spec = pl.BlockSpec((None, None, tm, d),
                              lambda bi, i: (bi, i // nbh, jnp.where(i < nbh, i, 2 * nbh - 1 - i), 0))
    else:
        h_spec = pl.BlockSpec((None, tm, d), lambda bi, i: (bi, i, 0))
    hr = 2 * SUBLANES
    tph = tm // hr
    tile = pl.BlockSpec((None, tm, dff), lambda bi, i: (bi, i, 0))
    halo_prev = pl.BlockSpec((None, hr, dff), lambda bi, i: (bi, jnp.maximum(i * tph - 1, 0), 0))
    halo_next = pl.BlockSpec((None, hr, dff), lambda bi, i: (bi, jnp.minimum((i + 1) * tph, s // hr - 1), 0))
    const = lambda shape: pl.BlockSpec(shape, lambda bi, i: (0,) * len(shape), pipeline_mode=pl.Buffered(1))
    return pl.pallas_call(
        functools.partial(_ffn_out_kernel, final_norm=final_norm, ck=ck),
        grid=(b, nt),
        in_specs=[tile, tile, halo_prev, halo_next,
                  pl.BlockSpec((3, dff), lambda bi, i: (0, 1), pipeline_mode=pl.Buffered(1)),
                  pl.BlockSpec((1, dff), lambda bi, i: (0, 1), pipeline_mode=pl.Buffered(1)),
                  const((dff, d)), h_spec, const((1, d))],
        out_specs=pl.BlockSpec((None, tm, d), lambda bi, i: (bi, i, 0)),
        out_shape=jax.ShapeDtypeStruct((b, s, d), F32),
        scratch_shapes=[pltpu.VMEM((tm, dff), BF16)],
        compiler_params=_params(("parallel", "arbitrary")),
        name="convffn_out_final" if final_norm else "convffn_out",
    )(sg, uv, uv, uv, conv_w, conv_b, w_out.astype(BF16), h, gf)


def _rwkv_pre_kernel(h_ref, hp_ref, hx_ref, g_ref, mu_ref, wrkv_ref, w1_ref, w2_ref, w0_ref,
                     a1_ref, a2_ref, a0_ref, g1_ref, g2_ref, kk_ref, ka_ref, rk_ref, hs_ref, he_ref,
                     r_out, v_out, kkn_out, lw_out, kd_out, bd_out, bonus_out, gate_out):
    i = pl.program_id(1)
    g = g_ref[...]
    hn = _rms(h_ref[...], g)
    ts = hn.shape[0]
    prev_edge = _rms(hp_ref[...], g)[SUBLANES - 1:SUBLANES]
    next_edge = _rms(hx_ref[...], g)[0:1]
    prev_edge = jnp.where(i == 0, 0.0, prev_edge)
    next_edge = jnp.where(i == pl.num_programs(1) - 1, 0.0, next_edge)
    row = lax.broadcasted_iota(jnp.int32, (ts, 1), 0)
    prev = jnp.where(row == 0, prev_edge, pltpu.roll(hn, 1, 0))
    nxt = jnp.where(row == ts - 1, next_edge, pltpu.roll(hn, ts - 1, 0))
    dp = prev - hn
    dn = nxt - hn
    mu = mu_ref[...]
    n_mix = mu.shape[0] // 2

    rb = ts // PRE_SUB_BLOCKS if ts % PRE_SUB_BLOCKS == 0 else ts
    subs = [slice(j * rb, (j + 1) * rb) for j in range(ts // rb)]

    def mix(m):
        return [(hn[sb] + dp[sb] * mu[m:m + 1] + dn[sb] * mu[n_mix + m:n_mix + m + 1]).astype(BF16) for sb in subs]

    r = [_dot(x, wrkv_ref[0]) for x in mix(0)]
    k = [_dot(x, wrkv_ref[1]) for x in mix(1)]
    v = [_dot(x, wrkv_ref[2]) for x in mix(2)]
    tw = [jnp.tanh(_dot(x, w1_ref[...])).astype(BF16) for x in mix(3)]
    ta = [_dot(x, a1_ref[...]).astype(BF16) for x in mix(4)]
    tg = [jax.nn.sigmoid(_dot(x, g1_ref[...])).astype(BF16) for x in mix(5)]
    gate = [_dot(x, g2_ref[...]) for x in tg]

    kraw = [x * kk_ref[...] for x in k]
    ssq = [_head_reduce(x * x, hs_ref, he_ref) for x in kraw]
    kkn = [x * lax.rsqrt(jnp.maximum(q, L2_EPS * L2_EPS)) for x, q in zip(kraw, ssq)]
    kka = [x * ka_ref[...] for x in k]
    kd_base = [x - y for x, y in zip(k, kka)]
    kd_sum = [jnp.zeros_like(x) for x in k]
    for zdir in range(2):
        xw = [w0_ref[zdir:zdir + 1] + _dot(x, w2_ref[zdir]) for x in tw]
        la = [_dot(x, a2_ref[zdir]) for x in ta]
        for j, sb in enumerate(subs):
            lw_out[zdir, sb, :] = jax.nn.sigmoid(xw[j]) * (-math.exp(-0.5) * math.log2(math.e))
            ag = jax.nn.sigmoid(a0_ref[zdir:zdir + 1] + la[j])
            kd = kd_base[j] + kka[j] * ag
            kd_sum[j] = kd_sum[j] + kd
            kd_out[zdir, sb, :] = kd.astype(BF16)
            bd_out[zdir, sb, :] = (kkn[j] * ag).astype(BF16)
    bsum = [_head_reduce(r[j] * kd_sum[j] * rk_ref[...], hs_ref, he_ref) for j in range(len(subs))]
    for j, sb in enumerate(subs):
        r_out[sb, :] = r[j].astype(BF16)
        v_out[sb, :] = v[j].astype(BF16)
        kkn_out[sb, :] = kkn[j].astype(BF16)
        bonus_out[sb, :] = (bsum[j] * v[j]).astype(bonus_out.dtype)
        gate_out[sb, :] = gate[j].astype(gate_out.dtype)


def _rwkv_pre(h, g, mu, w_rkv, w0, w1, w2, a0, a1, a2, g1, g2, k_k, k_a, r_k, hs, he):
    b, s, d = h.shape
    ts = min(PRE_ROWS, s)
    nt = s // ts
    nh8 = s // SUBLANES
    rdec = w1.shape[-1]
    raaa = a1.shape[-1]
    w1c = jnp.concatenate([w1[0], w1[1]], axis=1).astype(BF16)
    a1c = jnp.concatenate([a1[0], a1[1]], axis=1).astype(BF16)
    zw = jnp.zeros_like(w2[0])
    za = jnp.zeros_like(a2[0])
    w2p = jnp.stack([jnp.concatenate([w2[0], zw], 0), jnp.concatenate([zw, w2[1]], 0)]).astype(BF16)
    a2p = jnp.stack([jnp.concatenate([a2[0], za], 0), jnp.concatenate([za, a2[1]], 0)]).astype(BF16)
    full2 = lambda shape: pl.BlockSpec(shape, lambda bi, i: (0,) * len(shape), pipeline_mode=pl.Buffered(1))
    tile = pl.BlockSpec((None, ts, d), lambda bi, i: (bi, i, 0))
    tile2 = pl.BlockSpec((2, None, ts, d), lambda bi, i: (0, bi, i, 0))
    tpb = ts // SUBLANES
    in_specs = [
        tile,
        pl.BlockSpec((None, SUBLANES, d), lambda bi, i: (bi, jnp.maximum(i * tpb - 1, 0), 0)),
        pl.BlockSpec((None, SUBLANES, d), lambda bi, i: (bi, jnp.minimum((i + 1) * tpb, nh8 - 1), 0)),
        full2((1, d)), full2((2 * mu.shape[1], d)), full2((3, d, d)),
        full2((d, 2 * rdec)), full2((2, 2 * rdec, d)), full2((2, d)),
        full2((d, 2 * raaa)), full2((2, 2 * raaa, d)), full2((2, d)),
        full2(g1.shape), full2(g2.shape), full2((1, d)), full2((1, d)), full2((1, d)),
        full2(hs.shape), full2(he.shape),
    ]
    out_specs = [tile, tile, tile, tile2, tile2, tile2, tile, tile]
    bsd = (b, s, d)
    out_shape = [jax.ShapeDtypeStruct(bsd, BF16), jax.ShapeDtypeStruct(bsd, BF16),
                 jax.ShapeDtypeStruct(bsd, BF16), jax.ShapeDtypeStruct((2,) + bsd, F32),
                 jax.ShapeDtypeStruct((2,) + bsd, BF16), jax.ShapeDtypeStruct((2,) + bsd, BF16),
                 jax.ShapeDtypeStruct(bsd, BF16), jax.ShapeDtypeStruct(bsd, BF16)]
    return pl.pallas_call(
        _rwkv_pre_kernel,
        grid=(b, nt),
        in_specs=in_specs,
        out_specs=out_specs,
        out_shape=out_shape,
        compiler_params=_params(("parallel", "arbitrary")),
        name="rwkv_projections",
    )(h, h, h, g.reshape(1, d), mu.reshape(2 * mu.shape[1], d), w_rkv.astype(BF16),
      w1c, w2p, w0, a1c, a2p, a0, g1.astype(BF16), g2.astype(BF16),
      k_k.reshape(1, d), k_a.reshape(1, d), r_k.reshape(1, d), hs, he)


def _wkv_kernel(r_ref, v_ref, kk_ref, lw_ref, kd_ref, bd_ref, y_ref, st_ref, *, ch):
    zdir = pl.program_id(1)
    c = pl.program_id(2)
    gw = WKV_GROUP_LANES
    gh = gw // HEAD_DIM
    n_sub = r_ref.shape[0] // ch
    n_grp = r_ref.shape[1] // gw
    fwd = zdir == 0

    @pl.when(c == 0)
    def _():
        st_ref[...] = jnp.zeros_like(st_ref)

    sign = jnp.where(fwd, 1, -1)
    tt = lax.broadcasted_iota(jnp.int32, (ch, gh * ch), 0)
    jl = lax.broadcasted_iota(jnp.int32, (ch, gh * ch), 1)
    jj = jl - (jl // ch) * ch
    order = (tt - jj) * sign
    strict = order > 0
    incl = order >= 0
    eye = (tt == jj).astype(F32)
    r8 = lax.broadcasted_iota(jnp.int32, (ch, ch), 0)
    c8 = lax.broadcasted_iota(jnp.int32, (ch, ch), 1)
    tri = jnp.where((r8 - c8) * sign >= 0, 1.0, 0.0).astype(BF16)
    same_head = (lax.broadcasted_iota(jnp.int32, (gh * ch, gw), 0) // ch
                 == lax.broadcasted_iota(jnp.int32, (gh * ch, gw), 1) // HEAD_DIM)
    same_head_sq = (lax.broadcasted_iota(jnp.int32, (gw, gw), 0) // HEAD_DIM
                    == lax.broadcasted_iota(jnp.int32, (gw, gw), 1) // HEAD_DIM)

    def bd(x):
        xb = x.astype(BF16)
        return jnp.where(same_head, jnp.concatenate([xb] * gh, axis=0), jnp.zeros((), BF16))

    grps = range(n_grp)
    sls = [slice(g * gw, (g + 1) * gw) for g in grps]
    rows, g_tot, lr, bk, bkh, vw = [], [], [], [], [], []
    for k in range(n_sub):
        start = pl.multiple_of(jnp.where(fwd, k * ch, (n_sub - 1 - k) * ch), ch)
        rk = pl.ds(start, ch)
        rows.append(rk)
        lw = lw_ref[rk, :]
        lw_hi = lw.astype(BF16)
        lw_lo = (lw - lw_hi.astype(F32)).astype(BF16)
        cum = _dot(tri, lw_hi) + _dot(tri, lw_lo)
        tot = jnp.where(fwd, cum[ch - 1:ch], cum[0:1])
        e_neg = jnp.exp2(-cum)
        gk = jnp.exp2(tot)
        r_t = r_ref[rk, :].astype(F32) * jnp.exp2(cum)
        a_t = -(kk_ref[rk, :].astype(F32) * jnp.exp2(cum - lw))
        k_t = kd_ref[rk, :].astype(F32) * e_neg
        b_t = bd_ref[rk, :].astype(F32) * e_neg
        v_k = v_ref[rk, :]
        g_tot.append(gk)
        lr.append([jnp.concatenate([a_t[:, sl], r_t[:, sl]], axis=0).astype(BF16) for sl in sls])
        bk.append([jnp.concatenate([bd(b_t[:, sl]), bd(k_t[:, sl])], axis=0) for sl in sls])
        bkh.append([jnp.concatenate([b_t[:, sl], k_t[:, sl]], axis=0).astype(BF16) for sl in sls])
        vw.append([v_k[:, sl] for sl in sls])

    chains = [(k, g) for k in range(n_sub) for g in grps]
    n_levels = int(math.log2(ch))
    wd = gh * ch
    sc = {kg: _dot_nt(lr[kg[0]][kg[1]], bk[kg[0]][kg[1]]) for kg in chains}
    lab = {kg: jnp.where(strict, sc[kg][:ch, :wd], 0.0) for kg in chains}
    lak = {kg: jnp.where(strict, sc[kg][:ch, wd:], 0.0).astype(BF16) for kg in chains}
    mr = {kg: jnp.concatenate([jnp.where(incl, sc[kg][ch:, :wd], 0.0).astype(BF16),
                               jnp.where(incl, sc[kg][ch:, wd:], 0.0).astype(BF16)], axis=1) for kg in chains}
    vbd = {kg: bd(vw[kg[0]][kg[1]]) for kg in chains}
    t_inv = {kg: eye + lab[kg] for kg in chains}
    pk = {kg: _dot(lab[kg].astype(BF16), bd(lab[kg])) for kg in chains}
    for lvl in range(1, n_levels):
        pb = {kg: bd(pk[kg]) for kg in chains}
        if lvl < n_levels - 1:
            tp = {kg: _dot(jnp.concatenate([t_inv[kg], pk[kg]], axis=0).astype(BF16), pb[kg]) for kg in chains}
            t_inv = {kg: t_inv[kg] + tp[kg][:ch] for kg in chains}
            pk = {kg: tp[kg][ch:] for kg in chains}
        else:
            t_inv = {kg: t_inv[kg] + _dot(t_inv[kg].astype(BF16), pb[kg]) for kg in chains}
    t_b = {kg: t_inv[kg].astype(BF16) for kg in chains}
    lv = {kg: _dot(lak[kg], vbd[kg]) for kg in chains}

    st = [st_ref[g] for g in grps]
    for k in range(n_sub):
        lrs = [_dot_nt(lr[k][g], st[g].astype(BF16)) for g in grps]
        u = [_dot(t_b[(k, g)], bd(lrs[g][:ch] + lv[(k, g)])) for g in grps]
        uvb = [jnp.concatenate([bd(u[g]), vbd[(k, g)]], axis=0) for g in grps]
        for g in grps:
            y_ref[rows[k], sls[g]] = (lrs[g][ch:] + _dot(mr[(k, g)], uvb[g])).astype(y_ref.dtype)
        uv = [jnp.concatenate([u[g].astype(BF16), vw[k][g]], axis=0) for g in grps]
        st = [(st[g] + jnp.where(same_head_sq, _dot_tn(uv[g], bkh[k][g]), 0.0)) * g_tot[k][:, sls[g]] for g in grps]
    for g in grps:
        st_ref[g] = st[g]


def _wkv(r, v, kkn, lw, kd, bd):
    b, s, d = r.shape
    ch = min(WKV_CHUNK, s)
    rows = min(WKV_CHUNKS_PER_STEP * ch, s)
    nb = s // rows
    assert ch == HEAD_DIM, "wide-tile masks assume chunk length == head_dim"
    n_grp = d // WKV_GROUP_LANES

    def bidx(zdir, c):
        return jnp.where(zdir == 0, c, nb - 1 - c)

    shared = pl.BlockSpec((None, rows, d), lambda bi, zdir, c: (bi, bidx(zdir, c), 0))
    perdir = pl.BlockSpec((None, None, rows, d), lambda bi, zdir, c: (zdir, bi, bidx(zdir, c), 0))
    return pl.pallas_call(
        functools.partial(_wkv_kernel, ch=ch),
        grid=(b, 2, nb),
        in_specs=[shared, shared, shared, perdir, perdir, perdir],
        out_specs=perdir,
        out_shape=jax.ShapeDtypeStruct((2, b, s, d), BF16),
        scratch_shapes=[pltpu.VMEM((n_grp, WKV_GROUP_LANES, WKV_GROUP_LANES), F32)],
        compiler_params=_params(("parallel", "parallel", "arbitrary")),
        name="wkv7_chunked_scan",
    )(r, v, kkn, lw, kd, bd)


def _rwkv_post_kernel(y_ref, bonus_ref, gate_ref, h_ref, lnw_ref, lnb_ref, hs_ref, hem_ref, wo_ref, o_ref):
    ts = h_ref.shape[0]
    rb = ts // POST_SUB_BLOCKS if ts % POST_SUB_BLOCKS == 0 else ts
    subs = [pl.ds(i * rb, rb) for i in range(ts // rb)]
    y = [y_ref[0, r, :] + y_ref[1, r, :] for r in subs]
    mean = [_head_reduce(v, hs_ref, hem_ref) for v in y]
    dlt = [v - m for v, m in zip(y, mean)]
    var = [_head_reduce(v * v, hs_ref, hem_ref) for v in dlt]
    lnw = lnw_ref[...]
    lnb = lnb_ref[...]
    out = [((dv * lax.rsqrt(vr + GN_EPS) * lnw + lnb + bonus_ref[r, :]) * gate_ref[r, :]).astype(BF16)
           for dv, vr, r in zip(dlt, var, subs)]
    for v, r in zip(out, subs):
        o_ref[r, :] = h_ref[r, :] + _dot(v, wo_ref[...])


def _rwkv_post(y, bonus, gate, h, ln_w, ln_b, hs, hem, w_o):
    b, s, d = h.shape
    ts = min(POST_ROWS, s)
    tile = pl.BlockSpec((None, ts, d), lambda bi, i: (bi, i, 0))
    vec = pl.BlockSpec((1, d), lambda bi, i: (0, 0))
    mat = pl.BlockSpec((d, d), lambda bi, i: (0, 0))
    return pl.pallas_call(
        _rwkv_post_kernel,
        grid=(b, s // ts),
        in_specs=[pl.BlockSpec((2, None, ts, d), lambda bi, i: (0, bi, i, 0)),
                  tile, tile, tile, vec, vec,
                  pl.BlockSpec(hs.shape, lambda bi, i: (0, 0)), pl.BlockSpec(hem.shape, lambda bi, i: (0, 0)), mat],
        out_specs=tile,
        out_shape=jax.ShapeDtypeStruct((b, s, d), F32),
        compiler_params=_params(("parallel", "arbitrary")),
        name="rwkv_groupnorm_proj",
    )(y, bonus, gate, h, ln_w.reshape(1, d), ln_b.reshape(1, d), hs, hem, w_o.astype(BF16))


def _rwkv_layer(h, g, mu, w_rkv, w0, w1, w2, a0, a1, a2, g1, g2, k_k, k_a, r_k, ln_w, ln_b, w_o):
    d = h.shape[-1]
    head_of = np.arange(d) // HEAD_DIM
    ind = (head_of[:, None] == np.arange(LANES)[None, :]).astype(np.float32)
    hs = jnp.asarray(ind).astype(BF16)
    he = jnp.asarray(ind.T).astype(BF16)
    hem = jnp.asarray(ind.T / HEAD_DIM).astype(BF16)
    r, v, kkn, lw, kd, bd, bonus, gate = _rwkv_pre(
        h, g, mu, w_rkv, w0, w1, w2, a0, a1, a2, g1, g2, k_k, k_a, r_k, hs, he)
    y = _wkv(r, v, kkn, lw, kd, bd)
    return _rwkv_post(y, bonus, gate, h, ln_w, ln_b, hs, hem, w_o)


def kernel(x, norm_mix_g, norm_ffn_g, norm_final_g, fnet_w_o, fnet_b_o, rwkv_mu, rwkv_w_rkv, rwkv_w0, rwkv_w1, rwkv_w2, rwkv_a0, rwkv_a1, rwkv_a2, rwkv_g1, rwkv_g2, rwkv_k_k, rwkv_k_a, rwkv_r_k, rwkv_ln_w, rwkv_ln_b, rwkv_w_o, ffn_w_in, ffn_conv_w, ffn_conv_b, ffn_w_out):
    depth = norm_mix_g.shape[0]
    h = x
    for i in range(depth):
        j = i // 2
        if i % 2 == 0:
            h = _fnet_layer(h, norm_mix_g[i], fnet_w_o[j], fnet_b_o[j])
        else:
            h = _rwkv_layer(h, norm_mix_g[i], rwkv_mu[j], rwkv_w_rkv[j], rwkv_w0[j], rwkv_w1[j],
                            rwkv_w2[j], rwkv_a0[j], rwkv_a1[j], rwkv_a2[j], rwkv_g1[j], rwkv_g2[j],
                            rwkv_k_k[j], rwkv_k_a[j], rwkv_r_k[j], rwkv_ln_w[j], rwkv_ln_b[j],
                            rwkv_w_o[j])
        g_final = norm_final_g if i == depth - 1 else None
        h = _convffn_layer(h, norm_ffn_g[i], ffn_w_in, i, ffn_conv_w[i], ffn_conv_b[i],
                           ffn_w_out[i], g_final, mirrored=(i % 2 == 0))
    return h
```
